```python
import math
import jax
import jax.numpy as jnp
from jax import lax
import numpy as np

D_MODEL = 2048
BATCH = 4
SEQ = 8192
DEPTH = 4

GRID_W = 64
CTX_LEN = 256
N_MIXERS = 3
N_MOD = 9
RMS_EPS = 1e-6
D_FF = 5632

SSM_INNER = 2 * D_MODEL
SSM_HEAD_DIM = 64
SSM_HEADS = SSM_INNER // SSM_HEAD_DIM
SSM_GROUPS = 8
SSM_STATE = 128
SSM_CONV_W = 7
SSM_CHUNK = 128
SSM_GN = SSM_GROUPS * SSM_STATE
SSM_CONV_DIM = SSM_INNER + 2 * SSM_GN
SSM_PROJ = SSM_INNER + SSM_CONV_DIM + 2 * SSM_HEADS

POOL_WINDOWS = (2, 4, 8, 16)
POOL_GROUPS = 4
POOL_GROUP_DIM = D_MODEL // POOL_GROUPS

HEAD_DIM = 128
N_HEADS = D_MODEL // HEAD_DIM
N_KV_HEADS = 4
Q_PER_KV = N_HEADS // N_KV_HEADS
Q_DIM = N_HEADS * HEAD_DIM
KV_DIM = N_KV_HEADS * HEAD_DIM
WINDOW = 128
ATT_BLOCK = 128
ATT_SCALE = HEAD_DIM ** -0.5
ROPE_BASE = 10000.0
ROPE_AXIS_DIM = HEAD_DIM // 2

N_SSM_LAYERS = (DEPTH + 2) // 3
N_POOL_LAYERS = (DEPTH + 1) // 3
N_ATTN_LAYERS = DEPTH // 3

kernel_name = 'hybrid_ssd_pool_swa_diffusion_trunk'


def rmsnorm(h, g):
    hf = h.astype(jnp.float32)
    hf = hf * lax.rsqrt(jnp.mean(hf * hf, axis=-1, keepdims=True) + RMS_EPS)
    return (hf * g.astype(jnp.float32)).astype(h.dtype)


def adaln_params(cond, w, b):
    m = jnp.einsum('...d,de->...e', jax.nn.silu(cond), w) + b
    return m.reshape(m.shape[:-1] + (N_MOD, D_MODEL))


def modulate(h, g, m, k):
    return rmsnorm(h, g) * (1.0 + m[..., 3 * k + 1, :]) + m[..., 3 * k, :]


def swiglu(a, w_in, w_out):
    gu = jnp.einsum('bld,df->blf', a, w_in)
    return jnp.einsum('blf,fd->bld', jax.nn.silu(gu[..., :D_FF]) * gu[..., D_FF:], w_out)


def depthwise_conv(t, w, b):
    out = lax.conv_general_dilated(
        t, w[:, None, :].astype(t.dtype), window_strides=(1,),
        padding=[(SSM_CONV_W // 2, SSM_CONV_W // 2)],
        dimension_numbers=('NWC', 'WIO', 'NWC'), feature_group_count=t.shape[-1])
    return out + b.astype(t.dtype)


def ssd_chunked(xs, dt, a, bs, cs, init_state):
    bsz, n = xs.shape[:2]
    nc = n // SSM_CHUNK
    hg = SSM_HEADS // SSM_GROUPS
    shp = (bsz, nc, SSM_CHUNK, SSM_GROUPS, hg)
    xd = (xs.astype(jnp.float32) * dt[..., None]).reshape(shp + (SSM_HEAD_DIM,))
    acs = jnp.cumsum((dt * a).reshape(shp), axis=2)
    bq = bs.reshape(bsz, nc, SSM_CHUNK, SSM_GROUPS, SSM_STATE)
    cq = cs.reshape(bsz, nc, SSM_CHUNK, SSM_GROUPS, SSM_STATE)
    lower = jnp.tril(jnp.ones((SSM_CHUNK, SSM_CHUNK), dtype=bool))[None, None, :, :, None, None]
    seg = acs[:, :, :, None] - acs[:, :, None, :]
    decay = jnp.exp(jnp.where(lower, seg, -jnp.inf))
    cb = jnp.einsum('bclgn,bcsgn->bclsg', cq, bq)
    y_diag = jnp.einsum('bclsgh,bcsghp->bclghp', cb[..., None] * decay, xd)
    decay_to_end = jnp.exp(acs[:, :, -1:] - acs)
    chunk_states = jnp.einsum('bclgn,bclghp->bcghpn', bq, xd * decay_to_end[..., None])
    chunk_decay = jnp.exp(acs[:, :, -1])

    def step(state, inp):
        dec, st = inp
        return state * dec[..., None, None] + st, state

    final, states_in = lax.scan(
        step, init_state.reshape(bsz, SSM_GROUPS, hg, SSM_HEAD_DIM, SSM_STATE),
        (jnp.moveaxis(chunk_decay, 1, 0), jnp.moveaxis(chunk_states, 1, 0)))
    states_in = jnp.moveaxis(states_in, 0, 1)
    y_off = jnp.einsum('bclgn,bcghpn->bclghp', cq, states_in) * jnp.exp(acs)[..., None]
    y = (y_diag + y_off).reshape(bsz, n, SSM_HEADS, SSM_HEAD_DIM)
    return y, final.reshape(bsz, SSM_HEADS, SSM_HEAD_DIM, SSM_STATE)


def gated_group_rmsnorm(y, z, g):
    u = y * jax.nn.silu(z.astype(jnp.float32))
    shp = u.shape
    u = u.reshape(shp[:-1] + (SSM_GROUPS, SSM_INNER // SSM_GROUPS))
    u = u * lax.rsqrt(jnp.mean(u * u, axis=-1, keepdims=True) + RMS_EPS)
    return u.reshape(shp) * g.astype(jnp.float32)


def mamba_mixer(a_lat, a_ctx, w_in, conv_w, conv_b, dt_bias, a_log, d_skip, gn_g, w_out, need_ctx):
    def project(a):
        bsz, n = a.shape[:2]
        zxbcdt = jnp.einsum('bld,de->ble', a, w_in)
        z = zxbcdt[..., :SSM_INNER]
        xbc = jax.nn.silu(depthwise_conv(zxbcdt[..., SSM_INNER:SSM_INNER + SSM_CONV_DIM], conv_w, conv_b))
        dtr = zxbcdt[..., SSM_INNER + SSM_CONV_DIM:]
        xs = xbc[..., :SSM_INNER].reshape(bsz, n, SSM_HEADS, SSM_HEAD_DIM)
        bs = xbc[..., SSM_INNER:SSM_INNER + SSM_GN].reshape(bsz, n, SSM_GROUPS, SSM_STATE)
        cs = xbc[..., SSM_INNER + SSM_GN:].reshape(bsz, n, SSM_GROUPS, SSM_STATE)
        dt = jax.nn.softplus(dtr.reshape(bsz, n, 2, SSM_HEADS).astype(jnp.float32)
                             + dt_bias.astype(jnp.float32))
        return z, xs, bs, cs, dt

    zc, xc, bc, cc, dtc = project(a_ctx)
    zl, xl, bl, cl, dtl = project(a_lat)
    a = -jnp.exp(a_log.astype(jnp.float32))
    flip = lambda t: jnp.flip(t, axis=1)
    zero = jnp.zeros((a_lat.shape[0], SSM_HEADS, SSM_HEAD_DIM, SSM_STATE), jnp.float32)
    yc_f, sc_f = ssd_chunked(xc, dtc[:, :, 0], a[0], bc, cc, zero)
    yl_f, _ = ssd_chunked(xl, dtl[:, :, 0], a[0], bl, cl, sc_f)
    yc_b, sc_b = ssd_chunked(flip(xc), flip(dtc[:, :, 1]), a[1], flip(bc), flip(cc), zero)
    yl_b, _ = ssd_chunked(flip(xl), flip(dtl[:, :, 1]), a[1], flip(bl), flip(cl), sc_b)

    def finish(y_f, y_b_rev, xs, z):
        bsz, n = xs.shape[:2]
        y = y_f + flip(y_b_rev) + d_skip.astype(jnp.float32)[:, None] * xs.astype(jnp.float32)
        y = gated_group_rmsnorm(y.reshape(bsz, n, SSM_INNER), z, gn_g)
        return jnp.einsum('ble,ed->bld', y.astype(w_out.dtype), w_out)

    out_lat = finish(yl_f, yl_b, xl, zl).astype(a_lat.dtype)
    out_ctx = finish(yc_f, yc_b, xc, zc).astype(a_ctx.dtype) if need_ctx else None
    return out_lat, out_ctx


def multiscale_pool(a, w_grp, scale):
    bsz, n, _ = a.shape
    af = a.astype(jnp.float32)
    cs = jnp.concatenate([jnp.zeros((bsz, 1, D_MODEL), jnp.float32), jnp.cumsum(af, axis=1)], axis=1)
    t = jnp.arange(n)
    outs = []
    for g, w in enumerate(POOL_WINDOWS):
        lo = jnp.clip(t - w // 2, 0, n)
        hi = jnp.clip(t - w // 2 + w, 0, n)
        sl = slice(g * POOL_GROUP_DIM, (g + 1) * POOL_GROUP_DIM)
        win_sum = cs[:, hi, sl] - cs[:, lo, sl]
        outs.append(win_sum / (hi - lo).astype(jnp.float32)[:, None] - af[:, :, sl])
    pooled = jnp.stack(outs, axis=2)
    y = jnp.einsum('blgc,gce->blge', pooled, w_grp.astype(jnp.float32)).reshape(bsz, n, D_MODEL)
    return (y * scale.astype(jnp.float32)).astype(a.dtype)


def axial_rope_tables(n_tokens):
    rows = n_tokens // GRID_W
    row = jnp.repeat(jnp.arange(rows, dtype=jnp.int32), GRID_W)
    col = jnp.tile(jnp.arange(GRID_W, dtype=jnp.int32), rows)
    inv_freq = ROPE_BASE ** (-jnp.arange(0, ROPE_AXIS_DIM, 2, dtype=jnp.float32) / ROPE_AXIS_DIM)
    ang = jnp.concatenate([row.astype(jnp.float32)[:, None] * inv_freq,
                           col.astype(jnp.float32)[:, None] * inv_freq], axis=-1)
    return jnp.cos(ang), jnp.sin(ang)


def apply_rope(x, cos, sin):
    xp = x.astype(jnp.float32).reshape(x.shape[:-1] + (HEAD_DIM // 2, 2))
    x0, x1 = xp[..., 0], xp[..., 1]
    bshape = (x.shape[1],) + (1,) * (x.ndim - 3) + (HEAD_DIM // 2,)
    c, s = cos.reshape(bshape), sin.reshape(bshape)
    out = jnp.stack([x0 * c - x1 * s, x0 * s + x1 * c], axis=-1).reshape(x.shape)
    return out.astype(x.dtype)


def banded_gqa(q, k, v, k_ctx, v_ctx, sink):
    bsz, n = q.shape[:2]
    nb = n // ATT_BLOCK
    qb = q.reshape(bsz, nb, ATT_BLOCK, N_KV_HEADS, Q_PER_KV, HEAD_DIM)

    def band(t):
        tp = jnp.pad(t, ((0, 0), (ATT_BLOCK, ATT_BLOCK), (0, 0), (0, 0)))
        tp = tp.reshape(bsz, nb + 2, ATT_BLOCK, N_KV_HEADS, HEAD_DIM)
        return jnp.concatenate([tp[:, :-2], tp[:, 1:-1], tp[:, 2:]], axis=2)

    kb, vb = band(k), band(v)
    s_loc = jnp.einsum('bnqkgd,bnskd->bnkgqs', qb, kb).astype(jnp.float32) * ATT_SCALE
    blk = jnp.arange(nb)[:, None]
    q_pos = blk * ATT_BLOCK + jnp.arange(ATT_BLOCK)[None]
    k_pos = (blk - 1) * ATT_BLOCK + jnp.arange(3 * ATT_BLOCK)[None]
    rel = k_pos[:, None, :] - q_pos[:, :, None]
    valid = (jnp.abs(rel) <= WINDOW) & (k_pos[:, None, :] >= 0) & (k_pos[:, None, :] < n)
    s_loc = jnp.where(valid[None, :, None, None], s_loc, -jnp.inf)
    s_ctx = jnp.einsum('bnqkgd,bckd->bnkgqc', qb, k_ctx).astype(jnp.float32) * ATT_SCALE
    s_sink = jnp.broadcast_to(sink.astype(jnp.float32)[None, None, :, :, None, None], s_loc.shape[:-1] + (1,))
    p = jax.nn.softmax(jnp.concatenate([s_sink, s_ctx, s_loc], axis=-1), axis=-1)
    n_ctx = k_ctx.shape[1]
    p_ctx = p[..., 1:1 + n_ctx].astype(v.dtype)
    p_loc = p[..., 1 + n_ctx:].astype(v.dtype)
    o = (jnp.einsum('bnkgqc,bckd->bnqkgd', p_ctx, v_ctx)
         + jnp.einsum('bnkgqs,bnskd->bnqkgd', p_loc, vb))
    return o.reshape(bsz, n, Q_DIM)


def context_gqa(q, k, v, sink):
    s = jnp.einsum('bqkgd,bckd->bkgqc', q, k).astype(jnp.float32) * ATT_SCALE
    s0 = jnp.broadcast_to(sink.astype(jnp.float32)[None, :, :, None, None], s.shape[:-1] + (1,))
    p = jax.nn.softmax(jnp.concatenate([s0, s], axis=-1), axis=-1)[..., 1:]
    o = jnp.einsum('bkgqc,bckd->bqkgd', p.astype(v.dtype), v)
    return o.reshape(o.shape[0], o.shape[1], Q_DIM)


def window_attention_mixer(a_lat, a_ctx, w_qkv, sink, w_o, cos, sin, need_ctx):
    def project(a):
        bsz, n = a.shape[:2]
        qkv = jnp.einsum('bld,de->ble', a, w_qkv)
        q = qkv[..., :Q_DIM].reshape(bsz, n, N_KV_HEADS, Q_PER_KV, HEAD_DIM)
        k = qkv[..., Q_DIM:Q_DIM + KV_DIM].reshape(bsz, n, N_KV_HEADS, HEAD_DIM)
        v = qkv[..., Q_DIM + KV_DIM:].reshape(bsz, n, N_KV_HEADS, HEAD_DIM)
        return q, k, v

    q_l, k_l, v_l = project(a_lat)
    q_c, k_c, v_c = project(a_ctx)
    q_l = apply_rope(q_l, cos, sin)
    k_l = apply_rope(k_l, cos, sin)
    sink_g = sink.reshape(N_KV_HEADS, Q_PER_KV)
    o_l = banded_gqa(q_l, k_l, v_l, k_c, v_c, sink_g)
    y_l = jnp.einsum('ble,ed->bld', o_l, w_o).astype(a_lat.dtype)
    y_c = None
    if need_ctx:
        y_c = jnp.einsum('ble,ed->bld', context_gqa(q_c, k_c, v_c, sink_g), w_o).astype(a_ctx.dtype)
    return y_l, y_c


def setup_inputs(seed: int = 0) -> dict:
    key = jax.random.key(seed)
    ks = jax.random.split(key, 24)
    f32 = jnp.float32

    def nrm(k, shape, s):
        return jax.random.normal(k, shape, f32) * s

    x = nrm(ks[0], (BATCH, SEQ, D_MODEL), 1.0)
    c = nrm(ks[1], (BATCH, D_MODEL), 1.0)
    ctx = nrm(ks[2], (BATCH, CTX_LEN, D_MODEL), 1.0)
    c_ctx = nrm(ks[3], (D_MODEL,), 1.0)
    ada_w = nrm(ks[4], (DEPTH, D_MODEL, N_MOD * D_MODEL), 0.5 * D_MODEL ** -0.5)
    ada_b = nrm(ks[5], (DEPTH, N_MOD * D_MODEL), 0.02)
    norm_g = 1.0 + nrm(ks[6], (DEPTH, 3, D_MODEL), 0.1)
    ffn_w_in = nrm(ks[7], (DEPTH, 2, D_MODEL, 2 * D_FF), D_MODEL ** -0.5)
    ffn_w_out = nrm(ks[8], (DEPTH, 2, D_FF, D_MODEL), D_FF ** -0.5)
    ssm_w_in = nrm(ks[9], (N_SSM_LAYERS, D_MODEL, SSM_PROJ), D_MODEL ** -0.5)
    ssm_conv_w = nrm(ks[10], (N_SSM_LAYERS, SSM_CONV_W, SSM_CONV_DIM), SSM_CONV_W ** -0.5)
    ssm_conv_b = nrm(ks[11], (N_SSM_LAYERS, SSM_CONV_DIM), 0.02)
    dt0 = jnp.exp(jax.random.uniform(ks[12], (N_SSM_LAYERS, 2, SSM_HEADS), f32,
                                     math.log(1e-3), math.log(1e-1)))
    ssm_dt_bias = dt0 + jnp.log(-jnp.expm1(-dt0))
    ssm_a_log = jnp.log(jax.random.uniform(ks[13], (N_SSM_LAYERS, 2, SSM_HEADS), f32, 1.0, 16.0))
    ssm_d = 1.0 + nrm(ks[14], (N_SSM_LAYERS, SSM_HEADS), 0.1)
    ssm_norm_g = 1.0 + nrm(ks[15], (N_SSM_LAYERS, SSM_INNER), 0.1)
    ssm_w_out = nrm(ks[16], (N_SSM_LAYERS, SSM_INNER, D_MODEL), SSM_INNER ** -0.5)
    pool_w = nrm(ks[17], (N_POOL_LAYERS, POOL_GROUPS, POOL_GROUP_DIM, POOL_GROUP_DIM), POOL_GROUP_DIM ** -0.5)
    pool_scale = 1.0 + nrm(ks[18], (N_POOL_LAYERS, D_MODEL), 0.1)
    attn_w_qkv = nrm(ks[19], (N_ATTN_LAYERS, D_MODEL, Q_DIM + 2 * KV_DIM), D_MODEL ** -0.5)
    attn_sink = nrm(ks[20], (N_ATTN_LAYERS, N_HEADS), 0.5)
    attn_w_o = nrm(ks[21], (N_ATTN_LAYERS, Q_DIM, D_MODEL), Q_DIM ** -0.5)
    final_g = 1.0 + nrm(ks[22], (D_MODEL,), 0.1)
    return {'x': x, 'c': c, 'ctx': ctx, 'c_ctx': c_ctx, 'ada_w': ada_w, 'ada_b': ada_b,
            'norm_g': norm_g, 'ffn_w_in': ffn_w_in, 'ffn_w_out': ffn_w_out,
            'ssm_w_in': ssm_w_in, 'ssm_conv_w': ssm_conv_w, 'ssm_conv_b': ssm_conv_b,
            'ssm_dt_bias': ssm_dt_bias, 'ssm_a_log': ssm_a_log, 'ssm_d': ssm_d,
            'ssm_norm_g': ssm_norm_g, 'ssm_w_out': ssm_w_out, 'pool_w': pool_w,
            'pool_scale': pool_scale, 'attn_w_qkv': attn_w_qkv, 'attn_sink': attn_sink,
            'attn_w_o': attn_w_o, 'final_g': final_g}


def reference(x, c, ctx, c_ctx, ada_w, ada_b, norm_g, ffn_w_in, ffn_w_out,
              ssm_w_in, ssm_conv_w, ssm_conv_b, ssm_dt_bias, ssm_a_log, ssm_d,
              ssm_norm_g, ssm_w_out, pool_w, pool_scale, attn_w_qkv, attn_sink,
              attn_w_o, final_g):
    n_lat = x.shape[1]
    cos, sin = axial_rope_tables(n_lat)
    h, hc = x, ctx
    for i in range(DEPTH):
        kind, j = i % N_MIXERS, i // N_MIXERS
        last = i == DEPTH - 1
        ctx_live = (not last) or kind != 1
        ml = adaln_params(c, ada_w[i], ada_b[i])[:, None]
        mc = adaln_params(c_ctx, ada_w[i], ada_b[i])
        h = h + 0.5 * ml[..., 2, :] * swiglu(modulate(h, norm_g[i, 0], ml, 0), ffn_w_in[i, 0], ffn_w_out[i, 0])
        if ctx_live:
            hc = hc + 0.5 * mc[..., 2, :] * swiglu(modulate(hc, norm_g[i, 0], mc, 0), ffn_w_in[i, 0], ffn_w_out[i, 0])
        a_l = modulate(h, norm_g[i, 1], ml, 1)
        if kind == 0:
            a_c = modulate(hc, norm_g[i, 1], mc, 1)
            y_l, y_c = mamba_mixer(a_l, a_c, ssm_w_in[j], ssm_conv_w[j], ssm_conv_b[j], ssm_dt_bias[j],
                                   ssm_a_log[j], ssm_d[j], ssm_norm_g[j], ssm_w_out[j], not last)
        elif kind == 1:
            y_l = multiscale_pool(a_l, pool_w[j], pool_scale[j])
            y_c = None if last else multiscale_pool(modulate(hc, norm_g[i, 1], mc, 1), pool_w[j], pool_scale[j])
        else:
            a_c = modulate(hc, norm_g[i, 1], mc, 1)
            y_l, y_c = window_attention_mixer(a_l, a_c, attn_w_qkv[j], attn_sink[j], attn_w_o[j], cos, sin, not last)
        h = h + ml[..., 5, :] * y_l
        h = h + 0.5 * ml[..., 8, :] * swiglu(modulate(h, norm_g[i, 2], ml, 2), ffn_w_in[i, 1], ffn_w_out[i, 1])
        if not last:
            hc = hc + mc[..., 5, :] * y_c
            hc = hc + 0.5 * mc[..., 8, :] * swiglu(modulate(hc, norm_g[i, 2], mc, 2), ffn_w_in[i, 1], ffn_w_out[i, 1])
    return rmsnorm(h, final_g)
```

```python
import functools

import jax
import jax.numpy as jnp
from jax import lax
from jax.experimental import pallas as pl
from jax.experimental.pallas import tpu as pltpu

D_MODEL = 2048
N_MOD = 9
RMS_EPS = 1e-6
D_FF = 5632
GRID_W = 64

SSM_INNER = 2 * D_MODEL
SSM_HEAD_DIM = 64
SSM_HEADS = SSM_INNER // SSM_HEAD_DIM
SSM_GROUPS = 8
SSM_STATE = 128
SSM_CONV_W = 7
SSM_CHUNK = 128
SSM_GN = SSM_GROUPS * SSM_STATE
SSM_CONV_DIM = SSM_INNER + 2 * SSM_GN
SSM_PROJ = SSM_INNER + SSM_CONV_DIM + 2 * SSM_HEADS
SSM_PAIRS = SSM_HEADS // 2
SSM_PAIRS_PER_GROUP = SSM_PAIRS // SSM_GROUPS

POOL_WINDOWS = (2, 4, 8, 16)
POOL_GROUPS = 4
POOL_GROUP_DIM = D_MODEL // POOL_GROUPS
POOL_HALO = 8

HEAD_DIM = 128
N_HEADS = D_MODEL // HEAD_DIM
N_KV_HEADS = 4
Q_PER_KV = N_HEADS // N_KV_HEADS
Q_DIM = N_HEADS * HEAD_DIM
KV_DIM = N_KV_HEADS * HEAD_DIM
QKV_DIM = Q_DIM + 2 * KV_DIM
WINDOW = 128
ATT_BLOCK = 128
ATT_SCALE = HEAD_DIM ** -0.5
ROPE_BASE = 10000.0
ROPE_AXIS_DIM = HEAD_DIM // 2

LANES = 128
SUBLANES = 8
VMEM_LIMIT_BYTES = 56 * 1024 * 1024

F32 = jnp.float32
BF16 = jnp.bfloat16


def _params(*semantics):
    return pltpu.CompilerParams(dimension_semantics=semantics, vmem_limit_bytes=VMEM_LIMIT_BYTES)


def _tile(n, preferred):
    t = min(n, preferred)
    while n % t:
        t //= 2
    return t


def _dot(a, b):
    return jnp.dot(a, b, preferred_element_type=F32)


def _dot_nt(a, b):
    return lax.dot_general(a, b, (((1,), (1,)), ((), ())), preferred_element_type=F32)


def _sigmoid(x):
    return 1.0 / (1.0 + jnp.exp(-x))


def _silu(x):
    return x * _sigmoid(x)


def _rms_mod(h, g, shift, scale):
    ms = jnp.mean(h * h, axis=-1, keepdims=True)
    return (h * lax.rsqrt(ms + RMS_EPS) * g) * (1.0 + scale) + shift


def _adaln_kernel(c_ref, w_ref, b_ref, o_ref):
    s = _silu(c_ref[...]).astype(BF16)
    o_ref[0] = _dot(s, w_ref[0].astype(BF16)) + b_ref[0]


def adaln_all(cond, ada_w, ada_b):
    depth, _, n = ada_w.shape
    r = cond.shape[0]
    tn = _tile(n, 1024)
    return pl.pallas_call(
        _adaln_kernel,
        grid=(depth, n // tn),
        in_specs=[pl.BlockSpec((r, D_MODEL), lambda i, j: (0, 0)),
                  pl.BlockSpec((1, D_MODEL, tn), lambda i, j: (i, 0, j)),
                  pl.BlockSpec((1, 1, tn), lambda i, j: (i, 0, j))],
        out_specs=pl.BlockSpec((1, r, tn), lambda i, j: (i, 0, j)),
        out_shape=jax.ShapeDtypeStruct((depth, r, n), F32),
        compiler_params=_params("arbitrary", "arbitrary"),
        name="adaln",
    )(cond, ada_w, ada_b.reshape(depth, 1, n))


def _ffn_kernel(h_ref, mod_ref, g_ref, wg_ref, wu_ref, wo_ref, o_ref, a_ref, *, k, final_g):
    j = pl.program_id(1)

    @pl.when(j == 0)
    def _():
        a_ref[...] = _rms_mod(h_ref[...], g_ref[...], mod_ref[0, 3 * k:3 * k + 1, :],
                              mod_ref[0, 3 * k + 1:3 * k + 2, :]).astype(BF16)

    a = a_ref[...]
    act = (_silu(_dot(a, wg_ref[...])) * _dot(a, wu_ref[...])).astype(BF16)
    part = _dot(act, wo_ref[...])

    @pl.when(j == 0)
    def _():
        o_ref[...] = part

    @pl.when(j > 0)
    def _():
        o_ref[...] += part

    @pl.when(j == pl.num_programs(1) - 1)
    def _():
        o_ref[...] = h_ref[...] + (0.5 * mod_ref[0, 3 * k + 2:3 * k + 3, :]) * o_ref[...]


def ffn(h, mod, g, w_in, w_out, k, rows_per_mod):
    rows = h.shape[0]
    tm = _tile(rows_per_mod, 512)
    tf = _tile(D_FF, 512)
    nf = D_FF // tf
    per = rows_per_mod // tm
    return pl.pallas_call(
        functools.partial(_ffn_kernel, k=k, final_g=None),
        grid=(rows // tm, nf),
        in_specs=[pl.BlockSpec((tm, D_MODEL), lambda i, j: (i, 0)),
                  pl.BlockSpec((1, N_MOD, D_MODEL), lambda i, j: (i // per, 0, 0)),
                  pl.BlockSpec((1, D_MODEL), lambda i, j: (0, 0)),
                  pl.BlockSpec((D_MODEL, tf), lambda i, j: (0, j)),
                  pl.BlockSpec((D_MODEL, tf), lambda i, j: (0, nf + j)),
                  pl.BlockSpec((tf, D_MODEL), lambda i, j: (j, 0))],
        out_specs=pl.BlockSpec((tm, D_MODEL), lambda i, j: (i, 0)),
        out_shape=jax.ShapeDtypeStruct((rows, D_MODEL), F32),
        scratch_shapes=[pltpu.VMEM((tm, D_MODEL), BF16)],
        compiler_params=_params("arbitrary", "arbitrary"),
        name="ffn",
    )(h, mod, g.reshape(1, D_MODEL), w_in, w_in, w_out)


def _rope(x, cos, sin):
    lane = lax.broadcasted_iota(jnp.int32, x.shape, 1)
    partner = jnp.where(lane % 2 == 0, pltpu.roll(x, LANES - 1, 1), pltpu.roll(x, 1, 1))
    return x * cos + partner * sin


def _proj_kernel(h_ref, mod_ref, g_ref, w_ref, *rest, k, rope_tiles):
    if rope_tiles:
        cos_ref, sin_ref, o_ref, a_ref = rest
    else:
        o_ref, a_ref = rest
    j = pl.program_id(1)

    @pl.when(j == 0)
    def _():
        a_ref[...] = _rms_mod(h_ref[...], g_ref[...], mod_ref[0, 3 * k:3 * k + 1, :],
                              mod_ref[0, 3 * k + 1:3 * k + 2, :]).astype(BF16)

    res = _dot(a_ref[...], w_ref[...])
    if not rope_tiles:
        o_ref[...] = res.astype(o_ref.dtype)
        return

    @pl.when(j < rope_tiles)
    def _():
        cos, sin = cos_ref[...], sin_ref[...]
        for c in range(res.shape[1] // HEAD_DIM):
            sl = slice(c * HEAD_DIM, (c + 1) * HEAD_DIM)
            o_ref[:, sl] = _rope(res[:, sl], cos, sin).astype(o_ref.dtype)

    @pl.when(j >= rope_tiles)
    def _():
        o_ref[...] = res.astype(o_ref.dtype)


def mod_proj(h, mod, g, w, k, rows_per_mod, tn, out_dtype, rope=None):
    rows = h.shape[0]
    n = w.shape[1]
    tm = _tile(rows_per_mod, 1024)
    per = rows_per_mod // tm
    in_specs = [pl.BlockSpec((tm, D_MODEL), lambda i, j: (i, 0)),
                pl.BlockSpec((1, N_MOD, D_MODEL), lambda i, j: (i // per, 0, 0)),
                pl.BlockSpec((1, D_MODEL), lambda i, j: (0, 0)),
                pl.BlockSpec((D_MODEL, tn), lambda i, j: (0, j))]
    args = [h, mod, g.reshape(1, D_MODEL), w]
    rope_tiles = 0
    if rope is not None:
        cos, sin, n_cols = rope
        rope_tiles = n_cols // tn
        in_specs += [pl.BlockSpec((tm, HEAD_DIM), lambda i, j: (i % per, 0))] * 2
        args += [cos, sin]
    return pl.pallas_call(
        functools.partial(_proj_kernel, k=k, rope_tiles=rope_tiles),
        grid=(rows // tm, n // tn),
        in_specs=in_specs,
        out_specs=pl.BlockSpec((tm, tn), lambda i, j: (i, j)),
        out_shape=jax.ShapeDtypeStruct((rows, n), out_dtype),
        scratch_shapes=[pltpu.VMEM((tm, D_MODEL), BF16)],
        compiler_params=_params("arbitrary", "arbitrary"),
        name="mod_proj",
    )(*args)


def _out_proj_kernel(x_ref, w_ref, h_ref, mod_ref, o_ref):
    o_ref[...] = h_ref[...] + mod_ref[0, 5:6, :] * _dot(x_ref[...], w_ref[...])


def out_proj(x, w, h, mod, rows_per_mod):
    rows, kdim = x.shape
    tm = _tile(rows_per_mod, 512)
    tn = _tile(D_MODEL, 1024)
    per = rows_per_mod // tm
    return pl.pallas_call(
        _out_proj_kernel,
        grid=(D_MODEL // tn, rows // tm),
        in_specs=[pl.BlockSpec((tm, kdim), lambda j, i: (i, 0)),
                  pl.BlockSpec((kdim, tn), lambda j, i: (0, j)),
                  pl.BlockSpec((tm, tn), lambda j, i: (i, j)),
                  pl.BlockSpec((1, N_MOD, tn), lambda j, i: (i // per, 0, j))],
        out_specs=pl.BlockSpec((tm, tn), lambda j, i: (i, j)),
        out_shape=jax.ShapeDtypeStruct((rows, D_MODEL), F32),
        compiler_params=_params("arbitrary", "arbitrary"),
        name="out_proj",
    )(x, w, h, mod)


def _final_norm_kernel(h_ref, g_ref, o_ref):
    h = h_ref[...]
    o_ref[...] = h * lax.rsqrt(jnp.mean(h * h, axis=-1, keepdims=True) + RMS_EPS) * g_ref[...]


def final_norm(h, g):
    rows = h.shape[0]
    tm = _tile(rows, 512)
    return pl.pallas_call(
        _final_norm_kernel,
        grid=(rows // tm,),
        in_specs=[pl.BlockSpec((tm, D_MODEL), lambda i: (i, 0)),
                  pl.BlockSpec((1, D_MODEL), lambda i: (0, 0))],
        out_specs=pl.BlockSpec((tm, D_MODEL), lambda i: (i, 0)),
        out_shape=jax.ShapeDtypeStruct((rows, D_MODEL), F32),
        compiler_params=_params("arbitrary"),
        name="final_norm",
    )(h, g.reshape(1, D_MODEL))


def _conv_kernel(prev_ref, x_ref, next_ref, w_ref, b_ref, o_ref, buf_ref):
    t = pl.program_id(1)
    tr = x_ref.shape[1]
    halo = prev_ref.shape[1]
    buf_ref[0:halo, :] = jnp.where(t > 0, prev_ref[0], 0.0)
    buf_ref[halo:halo + tr, :] = x_ref[0]
    buf_ref[halo + tr:, :] = jnp.where(t < pl.num_programs(1) - 1, next_ref[0], 0.0)
    reach = SSM_CONV_W // 2
    acc = b_ref[...] + w_ref[0:1, :] * buf_ref[halo - reach:halo - reach + tr, :]
    for tap in range(1, SSM_CONV_W):
        lo = halo - reach + tap
        acc = acc + w_ref[tap:tap + 1, :] * buf_ref[lo:lo + tr, :]
    o_ref[0] = _silu(acc)


def conv_silu(zx, conv_w, conv_b):
    bsz, n, _ = zx.shape
    tr = _tile(n, 512)
    tc = 2048
    halo = SUBLANES
    col0 = SSM_INNER // tc
    nt = n // tr
    last_halo = n // halo - 1
    return pl.pallas_call(
        _conv_kernel,
        grid=(bsz, nt, SSM_CONV_DIM // tc),
        in_specs=[pl.BlockSpec((1, halo, tc), lambda b, t, c: (b, jnp.maximum(t * (tr // halo) - 1, 0), col0 + c)),
                  pl.BlockSpec((1, tr, tc), lambda b, t, c: (b, t, col0 + c)),
                  pl.BlockSpec((1, halo, tc),
                               lambda b, t, c: (b, jnp.minimum((t + 1) * (tr // halo), last_halo), col0 + c)),
                  pl.BlockSpec((SSM_CONV_W, tc), lambda b, t, c: (0, c)),
                  pl.BlockSpec((1, tc), lambda b, t, c: (0, c))],
        out_specs=pl.BlockSpec((1, tr, tc), lambda b, t, c: (b, t, c)),
        out_shape=jax.ShapeDtypeStruct((bsz, n, SSM_CONV_DIM), F32),
        scratch_shapes=[pltpu.VMEM((tr + 2 * halo, tc), F32)],
        compiler_params=_params("arbitrary", "arbitrary", "arbitrary"),
        name="conv_silu",
    )(zx, zx, zx, conv_w, conv_b.reshape(1, SSM_CONV_DIM))


def _ssd_kernel(x_ref, b_ref, c_ref, dtr_ref, dtb_ref, alog_ref, s0_ref, *rest, reverse):
    if reverse:
        yf_ref, z_ref, dskip_ref, gn_ref, y_ref, sfin_ref, st_ref = rest
    else:
        y_ref, sfin_ref, st_ref = rest
    c = pl.program_id(1)
    q = SSM_CHUNK

    @pl.when(c == 0)
    def _():
        st_ref[...] = s0_ref[0]

    pre = dtr_ref[0] + dtb_ref[...]
    dt = jnp.maximum(pre, 0.0) + jnp.log(1.0 + jnp.exp(-jnp.abs(pre)))
    da = dt * (-jnp.exp(alog_ref[...]))
    row = lax.broadcasted_iota(jnp.int32, (q, q), 0)
    col = lax.broadcasted_iota(jnp.int32, (q, q), 1)
    tri = (col >= row) if reverse else (col <= row)
    tri_bf = jnp.where(tri, 1.0, 0.0).astype(BF16)
    hi = da.astype(BF16)
    r1 = da - hi.astype(F32)
    mid = r1.astype(BF16)
    lo = (r1 - mid.astype(F32)).astype(BF16)
    cum = _dot(tri_bf, hi) + _dot(tri_bf, mid) + _dot(tri_bf, lo)
    cum_t = cum.T
    dt_t = dt.T
    head0 = SSM_HEADS if reverse else 0
    last = 0 if reverse else q - 1
    is_a = col < SSM_HEAD_DIM
    lane_row = lax.broadcasted_iota(jnp.int32, (1, LANES), 1)

    for g in range(SSM_GROUPS):
        gs = slice(g * SSM_STATE, (g + 1) * SSM_STATE)
        bg = b_ref[0, :, gs]
        cg = c_ref[0, :, gs]
        bg_t = bg.T
        cb = _dot_nt(cg.astype(BF16), bg.astype(BF16))
        outs = []
        for pp in range(SSM_PAIRS_PER_GROUP):
            p = g * SSM_PAIRS_PER_GROUP + pp
            ps = slice(p * LANES, (p + 1) * LANES)
            xp = x_ref[0, :, ps]
            stp = st_ref[p]
            y_pair = None
            upd = None
            decays = []
            for hh in range(2):
                hc = head0 + 2 * p + hh
                sel = is_a if hh == 0 else jnp.logical_not(is_a)
                cum_col = cum[:, hc:hc + 1]
                cum_row = cum_t[hc:hc + 1, :]
                dt_row = dt_t[hc:hc + 1, :]
                decay = jnp.where(tri, jnp.exp(jnp.where(tri, cum_col - cum_row, 0.0)), 0.0)
                m = (cb * decay * dt_row).astype(BF16)
                ce = (cg * jnp.exp(cum_col)).astype(BF16)
                xh = jnp.where(sel, xp, 0.0).astype(BF16)
                sth = jnp.where(sel, stp, 0.0).astype(BF16)
                yh = _dot(m, xh) + _dot(ce, sth)
                y_pair = yh if y_pair is None else y_pair + yh
                cum_last = cum_t[hc:hc + 1, last:last + 1]
                w_row = dt_row * jnp.exp(cum_last - cum_row)
                uh = _dot((bg_t * w_row).astype(BF16), xh)
                upd = uh if upd is None else upd + uh
                decays.append(jnp.exp(cum_last))
            chunk_decay = jnp.where(lane_row < SSM_HEAD_DIM, decays[0], decays[1])
            st_ref[p] = stp * chunk_decay + upd
            if reverse:
                y_tot = y_pair + yf_ref[0, :, ps] + dskip_ref[:, ps] * xp
                outs.append(y_tot * _silu(z_ref[0, :, ps]))
            else:
                y_ref[0, :, ps] = y_pair
        if reverse:
            ss = outs[0] * outs[0]
            for u in outs[1:]:
                ss = ss + u * u
            inv = lax.rsqrt(jnp.sum(ss, axis=-1, keepdims=True) / (SSM_PAIRS_PER_GROUP * LANES) + RMS_EPS)
            for pp, u in enumerate(outs):
                ps = slice((g * SSM_PAIRS_PER_GROUP + pp) * LANES, (g * SSM_PAIRS_PER_GROUP + pp + 1) * LANES)
                y_ref[0, :, ps] = (u * inv * gn_ref[:, ps]).astype(y_ref.dtype)

    @pl.when(c == pl.num_programs(1) - 1)
    def _():
        sfin_ref[0] = st_ref[...]


def ssd_scan(xbc, zx, dt_bias, a_log, state0, reverse, yf=None, d_skip=None, gn_g=None):
    bsz, n, _ = xbc.shape
    q = SSM_CHUNK
    nc = n // q
    cidx = (lambda c: nc - 1 - c) if reverse else (lambda c: c)
    dt_col = (SSM_INNER + SSM_CONV_DIM) // (2 * SSM_HEADS)
    state_spec = pl.BlockSpec((1, SSM_PAIRS, SSM_STATE, LANES), lambda b, c: (b, 0, 0, 0))
    in_specs = [pl.BlockSpec((1, q, SSM_INNER), lambda b, c: (b, cidx(c), 0)),
                pl.BlockSpec((1, q, SSM_GN), lambda b, c: (b, cidx(c), SSM_INNER // SSM_GN)),
                pl.BlockSpec((1, q, SSM_GN), lambda b, c: (b, cidx(c), SSM_INNER // SSM_GN + 1)),
                pl.BlockSpec((1, q, 2 * SSM_HEADS), lambda b, c: (b, cidx(c), dt_col)),
                pl.BlockSpec((1, 2 * SSM_HEADS), lambda b, c: (0, 0)),
                pl.BlockSpec((1, 2 * SSM_HEADS), lambda b, c: (0, 0)),
                state_spec]
    args = [xbc, xbc, xbc, zx, dt_bias.reshape(1, 2 * SSM_HEADS), a_log.reshape(1, 2 * SSM_HEADS), state0]
    if reverse:
        in_specs += [pl.BlockSpec((1, q, SSM_INNER), lambda b, c: (b, cidx(c), 0)),
                     pl.BlockSpec((1, q, SSM_INNER), lambda b, c: (b, cidx(c), 0)),
                     pl.BlockSpec((1, SSM_INNER), lambda b, c: (0, 0)),
                     pl.BlockSpec((1, SSM_INNER), lambda b, c: (0, 0))]
        args += [yf, zx, jnp.repeat(d_skip, SSM_HEAD_DIM).reshape(1, SSM_INNER), gn_g.reshape(1, SSM_INNER)]
    return pl.pallas_call(
        functools.partial(_ssd_kernel, reverse=reverse),
        grid=(bsz, nc),
        in_specs=in_specs,
        out_specs=[pl.BlockSpec((1, q, SSM_INNER), lambda b, c: (b, cidx(c), 0)), state_spec],
        out_shape=[jax.ShapeDtypeStruct((bsz, n, SSM_INNER), BF16 if reverse else F32),
                   jax.ShapeDtypeStruct((bsz, SSM_PAIRS, SSM_STATE, LANES), F32)],
        scratch_shapes=[pltpu.VMEM((SSM_PAIRS, SSM_STATE, LANES), F32)],
        compiler_params=_params("arbitrary", "arbitrary"),
        name="ssd_bwd" if reverse else "ssd_fwd",
    )(*args)


def _pool_kernel(prev_ref, h_ref, next_ref, mod_ref, g_ref, w_ref, sc_ref, o_ref, buf_ref, *, n_seq):
    t = pl.program_id(1)
    tr = h_ref.shape[1]
    halo = POOL_HALO
    g = g_ref[...]
    shift, scale = mod_ref[0, 3:4, :], mod_ref[0, 4:5, :]
    h = h_ref[0]
    buf_ref[0:halo, :] = jnp.where(t > 0, _rms_mod(prev_ref[0], g, shift, scale), 0.0)
    buf_ref[halo:halo + tr, :] = _rms_mod(h, g, shift, scale)
    buf_ref[halo + tr:, :] = jnp.where(t < pl.num_programs(1) - 1, _rms_mod(next_ref[0], g, shift, scale), 0.0)
    pos = t * tr + lax.broadcasted_iota(jnp.int32, (tr, 1), 0)
    for grp, win in enumerate(POOL_WINDOWS):
        cs = slice(grp * POOL_GROUP_DIM, (grp + 1) * POOL_GROUP_DIM)
        lo = halo - win // 2
        tot = buf_ref[lo:lo + tr, cs]
        for d in range(1, win):
            tot = tot + buf_ref[lo + d:lo + d + tr, cs]
        cnt = (jnp.minimum(pos - win // 2 + win, n_seq) - jnp.maximum(pos - win // 2, 0)).astype(F32)
        pooled = tot / cnt - buf_ref[halo:halo + tr, cs]
        y = _dot(pooled.astype(BF16), w_ref[grp]) * sc_ref[:, cs]
        o_ref[0, :, cs] = h[:, cs] + mod_ref[0, 5:6, cs] * y


def pool_mixer(h, mod, g, w_grp, scale):
    bsz, n, _ = h.shape
    tr = _tile(n, 256)
    halo = POOL_HALO
    last_halo = n // halo - 1
    per_batch = mod.shape[0] > 1
    mod_idx = (lambda b, t: (b, 0, 0)) if per_batch else (lambda b, t: (0, 0, 0))
    return pl.pallas_call(
        functools.partial(_pool_kernel, n_seq=n),
        grid=(bsz, n // tr),
        in_specs=[pl.BlockSpec((1, halo, D_MODEL), lambda b, t: (b, jnp.maximum(t * (tr // halo) - 1, 0), 0)),
                  pl.BlockSpec((1, tr, D_MODEL), lambda b, t: (b, t, 0)),
                  pl.BlockSpec((1, halo, D_MODEL),
                               lambda b, t: (b, jnp.minimum((t + 1) * (tr // halo), last_halo), 0)),
                  pl.BlockSpec((1, N_MOD, D_MODEL), mod_idx),
                  pl.BlockSpec((1, D_MODEL), lambda b, t: (0, 0)),
                  pl.BlockSpec((POOL_GROUPS, POOL_GROUP_DIM, POOL_GROUP_DIM), lambda b, t: (0, 0, 0)),
                  pl.BlockSpec((1, D_MODEL), lambda b, t: (0, 0))],
        out_specs=pl.BlockSpec((1, tr, D_MODEL), lambda b, t: (b, t, 0)),
        out_shape=jax.ShapeDtypeStruct((bsz, n, D_MODEL), F32),
        scratch_shapes=[pltpu.VMEM((tr + 2 * halo, D_MODEL), F32)],
        compiler_params=_params("arbitrary", "arbitrary"),
        name="pool_mixer",
    )(h, h, h, mod, g.reshape(1, D_MODEL), w_grp.astype(BF16), scale.reshape(1, D_MODEL))


def _attn_kernel(sink_ref, q_ref, kc_ref, vc_ref, *rest, band):
    if band:
        kp_ref, k0_ref, kn_ref, vp_ref, v0_ref, vn_ref, o_ref = rest
    else:
        (o_ref,) = rest
    blk = pl.program_id(1)
    kvh = pl.program_id(2)
    nb = pl.num_programs(1)
    qb = ATT_BLOCK
    q4 = jnp.concatenate([q_ref[0, :, g * HEAD_DIM:(g + 1) * HEAD_DIM] for g in range(Q_PER_KV)], axis=0)
    keys = [kc_ref[0]]
    vals = [vc_ref[0]]
    if band:
        keys += [kp_ref[0], k0_ref[0], kn_ref[0]]
        vals += [vp_ref[0], v0_ref[0], vn_ref[0]]
    k_all = jnp.concatenate(keys, axis=0) if band else keys[0]
    v_all = jnp.concatenate(vals, axis=0) if band else vals[0]
    s = _dot_nt(q4, k_all) * ATT_SCALE
    n_ctx = kc_ref.shape[1]
    if band:
        rows = lax.broadcasted_iota(jnp.int32, s.shape, 0) % qb
        cols = lax.broadcasted_iota(jnp.int32, s.shape, 1) - n_ctx
        diff = cols - rows
        ok = (cols < 0) | ((diff >= 0) & (diff <= 2 * WINDOW)
                           & ((cols >= qb) | (blk > 0)) & ((cols < 2 * qb) | (blk < nb - 1)))
        s = jnp.where(ok, s, -jnp.inf)
    sink = jnp.concatenate(
        [jnp.full((qb, 1), sink_ref[kvh * Q_PER_KV + g], F32) for g in range(Q_PER_KV)], axis=0)
    m = jnp.maximum(jnp.max(s, axis=-1, keepdims=True), sink)
    e = jnp.exp(s - m)
    denom = jnp.sum(e, axis=-1, keepdims=True) + jnp.exp(sink - m)
    p = (e / denom).astype(BF16)
    o4 = _dot(p, v_all)
    for g in range(Q_PER_KV):
        o_ref[0, :, g * HEAD_DIM:(g + 1) * HEAD_DIM] = o4[g * qb:(g + 1) * qb, :].astype(o_ref.dtype)


def attention(qkv, qkv_ctx, sink, band):
    bsz, n, _ = qkv.shape
    n_ctx = qkv_ctx.shape[1]
    qb = ATT_BLOCK
    nb = n // qb
    k_col = Q_DIM // HEAD_DIM
    v_col = (Q_DIM + KV_DIM) // HEAD_DIM
    in_specs = [pl.BlockSpec(memory_space=pltpu.SMEM),
                pl.BlockSpec((1, qb, Q_PER_KV * HEAD_DIM), lambda b, i, h: (b, i, h)),
                pl.BlockSpec((1, n_ctx, HEAD_DIM), lambda b, i, h: (b, 0, k_col + h)),
                pl.BlockSpec((1, n_ctx, HEAD_DIM), lambda b, i, h: (b, 0, v_col + h))]
    args = [sink, qkv, qkv_ctx, qkv_ctx]
    if band:
        for col in (k_col, v_col):
            in_specs += [
                pl.BlockSpec((1, qb, HEAD_DIM), lambda b, i, h, col=col: (b, jnp.maximum(i - 1, 0), col + h)),
                pl.BlockSpec((1, qb, HEAD_DIM), lambda b, i, h, col=col: (b, i, col + h)),
                pl.BlockSpec((1, qb, HEAD_DIM), lambda b, i, h, col=col: (b, jnp.minimum(i + 1, nb - 1), col + h))]
            args += [qkv, qkv, qkv]
    return pl.pallas_call(
        functools.partial(_attn_kernel, band=band),
        grid=(bsz, nb, N_KV_HEADS),
        in_specs=in_specs,
        out_specs=pl.BlockSpec((1, qb, Q_PER_KV * HEAD_DIM), lambda b, i, h: (b, i, h)),
        out_shape=jax.ShapeDtypeStruct((bsz, n, Q_DIM), BF16),
        compiler_params=_params("arbitrary", "arbitrary", "arbitrary"),
        name="attn_band" if band else "attn_ctx",
    )(*args)


def _rope_tables(n_tokens):
    rows = n_tokens // GRID_W
    row = jnp.repeat(jnp.arange(rows, dtype=jnp.int32), GRID_W)
    col = jnp.tile(jnp.arange(GRID_W, dtype=jnp.int32), rows)
    inv_freq = ROPE_BASE ** (-jnp.arange(0, ROPE_AXIS_DIM, 2, dtype=F32) / ROPE_AXIS_DIM)
    ang = jnp.concatenate([row.astype(F32)[:, None] * inv_freq, col.astype(F32)[:, None] * inv_freq], axis=-1)
    cos = jnp.repeat(jnp.cos(ang), 2, axis=-1)
    sin = jnp.stack([-jnp.sin(ang), jnp.sin(ang)], axis=-1).reshape(n_tokens, HEAD_DIM)
    return cos, sin


def kernel(x, c, ctx, c_ctx, ada_w, ada_b, norm_g, ffn_w_in, ffn_w_out, ssm_w_in, ssm_conv_w, ssm_conv_b,
           ssm_dt_bias, ssm_a_log, ssm_d, ssm_norm_g, ssm_w_out, pool_w, pool_scale, attn_w_qkv, attn_sink,
           attn_w_o, final_g):
    bsz, n_lat, _ = x.shape
    n_ctx = ctx.shape[1]
    depth = ada_w.shape[0]
    n_mixers = 3
    cond_rows = 2 * SUBLANES
    cond = jnp.zeros((cond_rows, D_MODEL), F32).at[:bsz].set(c).at[bsz].set(c_ctx)
    mods = adaln_all(cond, ada_w, ada_b)
    cos, sin = _rope_tables(n_lat)

    h = x.reshape(bsz * n_lat, D_MODEL)
    hc = ctx.reshape(bsz * n_ctx, D_MODEL)
    for i in range(depth):
        kind, j = i % n_mixers, i // n_mixers
        last = i == depth - 1
        ctx_live = (not last) or kind != 1
        ml = mods[i, :bsz].reshape(bsz, N_MOD, D_MODEL)
        mc = mods[i, bsz:bsz + 1].reshape(1, N_MOD, D_MODEL)
        w_in = [ffn_w_in[i, s].astype(BF16) for s in range(2)]
        w_out = [ffn_w_out[i, s].astype(BF16) for s in range(2)]

        h = ffn(h, ml, norm_g[i, 0], w_in[0], w_out[0], 0, n_lat)
        if ctx_live:
            hc = ffn(hc, mc, norm_g[i, 0], w_in[0], w_out[0], 0, bsz * n_ctx)

        if kind == 0:
            w_proj = ssm_w_in[j].astype(BF16)
            w_o = ssm_w_out[j].astype(BF16)
            tn = _tile(SSM_PROJ, 1152)
            zx_l = mod_proj(h, ml, norm_g[i, 1], w_proj, 1, n_lat, tn, F32).reshape(bsz, n_lat, SSM_PROJ)
            zx_c = mod_proj(hc, mc, norm_g[i, 1], w_proj, 1, bsz * n_ctx, tn, F32).reshape(bsz, n_ctx, SSM_PROJ)
            xbc_l = conv_silu(zx_l, ssm_conv_w[j], ssm_conv_b[j])
            xbc_c = conv_silu(zx_c, ssm_conv_w[j], ssm_conv_b[j])
            zero = jnp.zeros((bsz, SSM_PAIRS, SSM_STATE, LANES), F32)
            fin = dict(d_skip=ssm_d[j], gn_g=ssm_norm_g[j])
            yf_c, sf = ssd_scan(xbc_c, zx_c, ssm_dt_bias[j], ssm_a_log[j], zero, False)
            yn_c, sb = ssd_scan(xbc_c, zx_c, ssm_dt_bias[j], ssm_a_log[j], zero, True, yf=yf_c, **fin)
            yf_l, _ = ssd_scan(xbc_l, zx_l, ssm_dt_bias[j], ssm_a_log[j], sf, False)
            yn_l, _ = ssd_scan(xbc_l, zx_l, ssm_dt_bias[j], ssm_a_log[j], sb, True, yf=yf_l, **fin)
            h = out_proj(yn_l.reshape(bsz * n_lat, SSM_INNER), w_o, h, ml, n_lat)
            if not last:
                hc = out_proj(yn_c.reshape(bsz * n_ctx, SSM_INNER), w_o, hc, mc, bsz * n_ctx)
        elif kind == 1:
            h = pool_mixer(h.reshape(bsz, n_lat, D_MODEL), ml, norm_g[i, 1], pool_w[j],
                           pool_scale[j]).reshape(bsz * n_lat, D_MODEL)
            if not last:
                hc = pool_mixer(hc.reshape(bsz, n_ctx, D_MODEL), mc, norm_g[i, 1], pool_w[j],
                                pool_scale[j]).reshape(bsz * n_ctx, D_MODEL)
        else:
            w_qkv = attn_w_qkv[j].astype(BF16)
            w_o = attn_w_o[j].astype(BF16)
            tn = 512
            qkv_l = mod_proj(h, ml, norm_g[i, 1], w_qkv, 1, n_lat, tn, BF16,
                             rope=(cos, sin, Q_DIM + KV_DIM)).reshape(bsz, n_lat, QKV_DIM)
            qkv_c = mod_proj(hc, mc, norm_g[i, 1], w_qkv, 1, bsz * n_ctx, tn, BF16).reshape(bsz, n_ctx, QKV_DIM)
            o_l = attention(qkv_l, qkv_c, attn_sink[j], True)
            h = out_proj(o_l.reshape(bsz * n_lat, Q_DIM), w_o, h, ml, n_lat)
            if not last:
                o_c = attention(qkv_c, qkv_c, attn_sink[j], False)
                hc = out_proj(o_c.reshape(bsz * n_ctx, Q_DIM), w_o, hc, mc, bsz * n_ctx)

        h = ffn(h, ml, norm_g[i, 2], w_in[1], w_out[1], 2, n_lat)
        if not last:
            hc = ffn(hc, mc, norm_g[i, 2], w_in[1], w_out[1], 2, bsz * n_ctx)
    return final_norm(h, final_g).reshape(bsz, n_lat, D_MODEL)
```

```python
import functools

import jax
import jax.numpy as jnp
from jax import lax
from jax.experimental import pallas as pl
from jax.experimental.pallas import tpu as pltpu

D_MODEL = 2048
N_MOD = 9
RMS_EPS = 1e-6
D_FF = 5632
GRID_W = 64

SSM_INNER = 2 * D_MODEL
SSM_HEAD_DIM = 64
SSM_HEADS = SSM_INNER // SSM_HEAD_DIM
SSM_GROUPS = 8
SSM_STATE = 128
SSM_CONV_W = 7
SSM_CHUNK = 128
SSM_GN = SSM_GROUPS * SSM_STATE
SSM_CONV_DIM = SSM_INNER + 2 * SSM_GN
SSM_PROJ = SSM_INNER + SSM_CONV_DIM + 2 * SSM_HEADS
SSM_PAIRS = SSM_HEADS // 2
SSM_PAIRS_PER_GROUP = SSM_PAIRS // SSM_GROUPS
SSM_ZX = SSM_INNER + SSM_CONV_DIM
SSM_IN_ROWS = 1024
SSM_IN_COLS = 1024
CONV_HALO = 16

POOL_WINDOWS = (2, 4, 8, 16)
POOL_GROUPS = 4
POOL_GROUP_DIM = D_MODEL // POOL_GROUPS
POOL_HALO = 8

HEAD_DIM = 128
N_HEADS = D_MODEL // HEAD_DIM
N_KV_HEADS = 4
Q_PER_KV = N_HEADS // N_KV_HEADS
Q_DIM = N_HEADS * HEAD_DIM
KV_DIM = N_KV_HEADS * HEAD_DIM
QKV_DIM = Q_DIM + 2 * KV_DIM
WINDOW = 128
ATT_BLOCK = 128
ATT_SCALE = HEAD_DIM ** -0.5
ROPE_BASE = 10000.0
ROPE_AXIS_DIM = HEAD_DIM // 2

LANES = 128
SUBLANES = 8
VMEM_LIMIT_BYTES = 56 * 1024 * 1024

F32 = jnp.float32
BF16 = jnp.bfloat16


def _params(*semantics):
    return pltpu.CompilerParams(dimension_semantics=semantics, vmem_limit_bytes=VMEM_LIMIT_BYTES)


def _tile(n, preferred):
    t = min(n, preferred)
    while n % t:
        t //= 2
    return t


def _dot(a, b):
    return jnp.dot(a, b, preferred_element_type=F32)


def _dot_nt(a, b):
    return lax.dot_general(a, b, (((1,), (1,)), ((), ())), preferred_element_type=F32)


def _sigmoid(x):
    return 1.0 / (1.0 + jnp.exp(-x))


def _silu(x):
    return x * _sigmoid(x)


def _rms_mod(h, g, shift, scale):
    ms = jnp.mean(h * h, axis=-1, keepdims=True)
    return (h * lax.rsqrt(ms + RMS_EPS) * g) * (1.0 + scale) + shift


def _adaln_kernel(c_ref, w_ref, b_ref, o_ref):
    s = _silu(c_ref[...]).astype(BF16)
    o_ref[0] = _dot(s, w_ref[0].astype(BF16)) + b_ref[0]


def adaln_all(cond, ada_w, ada_b):
    depth, _, n = ada_w.shape
    r = cond.shape[0]
    tn = _tile(n, 1024)
    return pl.pallas_call(
        _adaln_kernel,
        grid=(depth, n // tn),
        in_specs=[pl.BlockSpec((r, D_MODEL), lambda i, j: (0, 0)),
                  pl.BlockSpec((1, D_MODEL, tn), lambda i, j: (i, 0, j)),
                  pl.BlockSpec((1, 1, tn), lambda i, j: (i, 0, j))],
        out_specs=pl.BlockSpec((1, r, tn), lambda i, j: (i, 0, j)),
        out_shape=jax.ShapeDtypeStruct((depth, r, n), F32),
        compiler_params=_params("arbitrary", "arbitrary"),
        name="adaln",
    )(cond, ada_w, ada_b.reshape(depth, 1, n))


def _ffn_kernel(h_ref, mod_ref, g_ref, wg_ref, wu_ref, wo_ref, *rest, k, final_norm):
    if final_norm:
        fg_ref, o_ref, a_ref = rest
    else:
        o_ref, a_ref = rest
    j = pl.program_id(1)

    @pl.when(j == 0)
    def _():
        a_ref[...] = _rms_mod(h_ref[...], g_ref[...], mod_ref[0, 3 * k:3 * k + 1, :],
                              mod_ref[0, 3 * k + 1:3 * k + 2, :]).astype(BF16)
        o_ref[...] = jnp.zeros_like(o_ref)

    a = a_ref[...]
    act = (_silu(_dot(a, wg_ref[...])) * _dot(a, wu_ref[...])).astype(BF16)
    o_ref[...] += _dot(act, wo_ref[...])

    @pl.when(j == pl.num_programs(1) - 1)
    def _():
        h = h_ref[...] + (0.5 * mod_ref[0, 3 * k + 2:3 * k + 3, :]) * o_ref[...]
        if final_norm:
            h = h * lax.rsqrt(jnp.mean(h * h, axis=-1, keepdims=True) + RMS_EPS) * fg_ref[...]
        o_ref[...] = h


def ffn(h, mod, g, w_in, w_out, k, rows_per_mod, final_g=None):
    rows = h.shape[0]
    tm = _tile(rows_per_mod, 512)
    tf = _tile(D_FF, 512)
    nf = D_FF // tf
    per = rows_per_mod // tm
    vec_spec = pl.BlockSpec((1, D_MODEL), lambda i, j: (0, 0))
    in_specs = [pl.BlockSpec((tm, D_MODEL), lambda i, j: (i, 0)),
                pl.BlockSpec((1, N_MOD, D_MODEL), lambda i, j: (i // per, 0, 0)),
                vec_spec,
                pl.BlockSpec((D_MODEL, tf), lambda i, j: (0, j)),
                pl.BlockSpec((D_MODEL, tf), lambda i, j: (0, nf + j)),
                pl.BlockSpec((tf, D_MODEL), lambda i, j: (j, 0))]
    args = [h, mod, g.reshape(1, D_MODEL), w_in, w_in, w_out]
    if final_g is not None:
        in_specs.append(vec_spec)
        args.append(final_g.reshape(1, D_MODEL))
    return pl.pallas_call(
        functools.partial(_ffn_kernel, k=k, final_norm=final_g is not None),
        grid=(rows // tm, nf),
        in_specs=in_specs,
        out_specs=pl.BlockSpec((tm, D_MODEL), lambda i, j: (i, 0)),
        out_shape=jax.ShapeDtypeStruct((rows, D_MODEL), F32),
        scratch_shapes=[pltpu.VMEM((tm, D_MODEL), BF16)],
        compiler_params=_params("arbitrary", "arbitrary"),
        name="ffn",
    )(*args)


def _rope(x, cos, sin):
    lane = lax.broadcasted_iota(jnp.int32, x.shape, 1)
    partner = jnp.where(lane % 2 == 0, pltpu.roll(x, LANES - 1, 1), pltpu.roll(x, 1, 1))
    return x * cos + partner * sin


def _proj_kernel(h_ref, mod_ref, g_ref, w_ref, *rest, k, rope_tiles):
    if rope_tiles:
        cos_ref, sin_ref, o_ref, a_ref = rest
    else:
        o_ref, a_ref = rest
    j = pl.program_id(1)

    @pl.when(j == 0)
    def _():
        a_ref[...] = _rms_mod(h_ref[...], g_ref[...], mod_ref[0, 3 * k:3 * k + 1, :],
                              mod_ref[0, 3 * k + 1:3 * k + 2, :]).astype(BF16)

    res = _dot(a_ref[...], w_ref[...])
    if not rope_tiles:
        o_ref[...] = res.astype(o_ref.dtype)
        return

    @pl.when(j < rope_tiles)
    def _():
        cos, sin = cos_ref[...], sin_ref[...]
        for c in range(res.shape[1] // HEAD_DIM):
            sl = slice(c * HEAD_DIM, (c + 1) * HEAD_DIM)
            o_ref[:, sl] = _rope(res[:, sl], cos, sin).astype(o_ref.dtype)

    @pl.when(j >= rope_tiles)
    def _():
        o_ref[...] = res.astype(o_ref.dtype)


def mod_proj(h, mod, g, w, k, rows_per_mod, tn, out_dtype, rope=None):
    rows = h.shape[0]
    n = w.shape[1]
    tm = _tile(rows_per_mod, 1024)
    per = rows_per_mod // tm
    in_specs = [pl.BlockSpec((tm, D_MODEL), lambda i, j: (i, 0)),
                pl.BlockSpec((1, N_MOD, D_MODEL), lambda i, j: (i // per, 0, 0)),
                pl.BlockSpec((1, D_MODEL), lambda i, j: (0, 0)),
                pl.BlockSpec((D_MODEL, tn), lambda i, j: (0, j))]
    args = [h, mod, g.reshape(1, D_MODEL), w]
    rope_tiles = 0
    if rope is not None:
        cos, sin, n_cols = rope
        rope_tiles = n_cols // tn
        in_specs += [pl.BlockSpec((tm, HEAD_DIM), lambda i, j: (i % per, 0))] * 2
        args += [cos, sin]
    return pl.pallas_call(
        functools.partial(_proj_kernel, k=k, rope_tiles=rope_tiles),
        grid=(rows // tm, n // tn),
        in_specs=in_specs,
        out_specs=pl.BlockSpec((tm, tn), lambda i, j: (i, j)),
        out_shape=jax.ShapeDtypeStruct((rows, n), out_dtype),
        scratch_shapes=[pltpu.VMEM((tm, D_MODEL), BF16)],
        compiler_params=_params("arbitrary", "arbitrary"),
        name="mod_proj",
    )(*args)


def _out_proj_kernel(x_ref, w_ref, h_ref, mod_ref, o_ref):
    o_ref[...] = h_ref[...] + mod_ref[0, 5:6, :] * _dot(x_ref[...], w_ref[...])


def out_proj(x, w, h, mod, rows_per_mod):
    rows, kdim = x.shape
    tm = _tile(rows_per_mod, 512)
    tn = _tile(D_MODEL, 1024)
    per = rows_per_mod // tm
    return pl.pallas_call(
        _out_proj_kernel,
        grid=(D_MODEL // tn, rows // tm),
        in_specs=[pl.BlockSpec((tm, kdim), lambda j, i: (i, 0)),
                  pl.BlockSpec((kdim, tn), lambda j, i: (0, j)),
                  pl.BlockSpec((tm, tn), lambda j, i: (i, j)),
                  pl.BlockSpec((1, N_MOD, tn), lambda j, i: (i // per, 0, j))],
        out_specs=pl.BlockSpec((tm, tn), lambda j, i: (i, j)),
        out_shape=jax.ShapeDtypeStruct((rows, D_MODEL), F32),
        compiler_params=_params("arbitrary", "arbitrary"),
        name="out_proj",
    )(x, w, h, mod)


def _ssm_in_kernel(prev_ref, h_ref, next_ref, mod_ref, g_ref, w_ref, wdt_ref, cw_ref, cb_ref,
                   o_ref, dt_ref, a_ref, res_ref, *, z_tiles):
    t = pl.program_id(1)
    j = pl.program_id(2)
    tm = h_ref.shape[1]
    halo = CONV_HALO
    n_ext = tm + 2 * halo

    @pl.when(j == 0)
    def _():
        g, shift, scale = g_ref[...], mod_ref[0, 3:4, :], mod_ref[0, 4:5, :]
        a_ref[0:halo, :] = jnp.where(t > 0, _rms_mod(prev_ref[0], g, shift, scale), 0.0).astype(BF16)
        a_ref[halo:halo + tm, :] = _rms_mod(h_ref[0], g, shift, scale).astype(BF16)
        a_ref[halo + tm:, :] = jnp.where(t < pl.num_programs(1) - 1,
                                         _rms_mod(next_ref[0], g, shift, scale), 0.0).astype(BF16)
        dt_ref[0] = _dot(a_ref[halo:halo + tm, :], wdt_ref[...])

    @pl.when(j < z_tiles)
    def _():
        o_ref[0] = _dot(a_ref[halo:halo + tm, :], w_ref[...])

    @pl.when(j >= z_tiles)
    def _():
        res_ref[...] = _dot(a_ref[...], w_ref[...])
        for c in range(o_ref.shape[2] // LANES):
            cs = slice(c * LANES, (c + 1) * LANES)
            x = res_ref[:, cs]
            w = cw_ref[:, cs]
            up = w[6:7] * x
            up = pltpu.roll(up, n_ext - 1, 0) + w[5:6] * x
            up = pltpu.roll(up, n_ext - 1, 0) + w[4:5] * x
            down = w[0:1] * x
            down = pltpu.roll(down, 1, 0) + w[1:2] * x
            down = pltpu.roll(down, 1, 0) + w[2:3] * x
            out = w[3:4] * x + pltpu.roll(up, n_ext - 1, 0) + pltpu.roll(down, 1, 0) + cb_ref[:, cs]
            o_ref[0, :, cs] = _silu(out[halo:halo + tm])


def ssm_in_proj(h, mod, g, w_zx, w_dt, conv_w, conv_b):
    bsz, n, _ = h.shape
    tm = _tile(n, SSM_IN_ROWS)
    tn = SSM_IN_COLS
    halo = CONV_HALO
    z_tiles = SSM_INNER // tn
    last_halo = n // halo - 1
    per_batch = mod.shape[0] > 1
    mod_idx = (lambda b, t, j: (b, 0, 0)) if per_batch else (lambda b, t, j: (0, 0, 0))
    conv_idx = lambda b, t, j: (0, jnp.maximum(j - z_tiles, 0))
    return pl.pallas_call(
        functools.partial(_ssm_in_kernel, z_tiles=z_tiles),
        grid=(bsz, n // tm, SSM_ZX // tn),
        in_specs=[pl.BlockSpec((1, halo, D_MODEL), lambda b, t, j: (b, jnp.maximum(t * (tm // halo) - 1, 0), 0)),
                  pl.BlockSpec((1, tm, D_MODEL), lambda b, t, j: (b, t, 0)),
                  pl.BlockSpec((1, halo, D_MODEL),
                               lambda b, t, j: (b, jnp.minimum((t + 1) * (tm // halo), last_halo), 0)),
                  pl.BlockSpec((1, N_MOD, D_MODEL), mod_idx),
                  pl.BlockSpec((1, D_MODEL), lambda b, t, j: (0, 0)),
                  pl.BlockSpec((D_MODEL, tn), lambda b, t, j: (0, j)),
                  pl.BlockSpec((D_MODEL, 2 * SSM_HEADS), lambda b, t, j: (0, 0)),
                  pl.BlockSpec((SSM_CONV_W, tn), conv_idx),
                  pl.BlockSpec((1, tn), conv_idx)],
        out_specs=[pl.BlockSpec((1, tm, tn), lambda b, t, j: (b, t, j)),
                   pl.BlockSpec((1, tm, 2 * SSM_HEADS), lambda b, t, j: (b, t, 0))],
        out_shape=[jax.ShapeDtypeStruct((bsz, n, SSM_ZX), F32),
                   jax.ShapeDtypeStruct((bsz, n, 2 * SSM_HEADS), F32)],
        scratch_shapes=[pltpu.VMEM((tm + 2 * halo, D_MODEL), BF16),
                        pltpu.VMEM((tm + 2 * halo, tn), F32)],
        compiler_params=_params("arbitrary", "arbitrary", "arbitrary"),
        name="ssm_in_proj",
    )(h, h, h, mod, g.reshape(1, D_MODEL), w_zx, w_dt, conv_w, conv_b.reshape(1, SSM_CONV_DIM))


def _ssd_kernel(x_ref, b_ref, c_ref, dtr_ref, dtb_ref, alog_ref, s0_ref, *rest, reverse):
    if reverse:
        yf_ref, z_ref, dskip_ref, gn_ref, y_ref, sfin_ref, st_ref = rest
    else:
        y_ref, sfin_ref, st_ref = rest
    c = pl.program_id(1)
    q = SSM_CHUNK

    @pl.when(c == 0)
    def _():
        st_ref[...] = s0_ref[0]

    pre = dtr_ref[0] + dtb_ref[...]
    dt = jnp.maximum(pre, 0.0) + jnp.log(1.0 + jnp.exp(-jnp.abs(pre)))
    da = dt * (-jnp.exp(alog_ref[...]))
    row = lax.broadcasted_iota(jnp.int32, (q, q), 0)
    col = lax.broadcasted_iota(jnp.int32, (q, q), 1)
    tri = (col >= row) if reverse else (col <= row)
    tri_bf = jnp.where(tri, 1.0, 0.0).astype(BF16)
    hi = da.astype(BF16)
    r1 = da - hi.astype(F32)
    mid = r1.astype(BF16)
    lo = (r1 - mid.astype(F32)).astype(BF16)
    cum = _dot(tri_bf, hi) + _dot(tri_bf, mid) + _dot(tri_bf, lo)
    cum_t = cum.T
    dt_t = dt.T
    head0 = SSM_HEADS if reverse else 0
    last = 0 if reverse else q - 1
    is_a = col < SSM_HEAD_DIM
    lane_row = lax.broadcasted_iota(jnp.int32, (1, LANES), 1)

    for g in range(SSM_GROUPS):
        gs = slice(g * SSM_STATE, (g + 1) * SSM_STATE)
        bg = b_ref[0, :, gs]
        cg = c_ref[0, :, gs]
        bg_t = bg.T
        cb = _dot_nt(cg.astype(BF16), bg.astype(BF16))
        outs = []
        for pp in range(SSM_PAIRS_PER_GROUP):
            p = g * SSM_PAIRS_PER_GROUP + pp
            ps = slice(p * LANES, (p + 1) * LANES)
            xp = x_ref[0, :, ps]
            stp = st_ref[p]
            x_heads = [jnp.where(is_a, xp, 0.0).astype(BF16), jnp.where(is_a, 0.0, xp).astype(BF16)]
            st_heads = [jnp.where(is_a, stp, 0.0).astype(BF16), jnp.where(is_a, 0.0, stp).astype(BF16)]
            ms, ces, ws, decays = [], [], [], []
            for hh in range(2):
                hc = head0 + 2 * p + hh
                cum_col = jnp.broadcast_to(cum[:, hc:hc + 1], (q, q))
                cum_row = cum_t[hc:hc + 1, :]
                dt_row = dt_t[hc:hc + 1, :]
                decay = jnp.exp(jnp.where(tri, cum_col - cum_row, -jnp.inf))
                ms.append((cb * decay * dt_row).astype(BF16))
                ces.append((cg * jnp.exp(cum_col)).astype(BF16))
                cum_last = cum_t[hc:hc + 1, last:last + 1]
                w_row = dt_row * jnp.exp(cum_last - cum_row)
                ws.append((bg_t * w_row).astype(BF16))
                decays.append(jnp.exp(cum_last))
            y_pair = _dot(jnp.concatenate(ms + ces, axis=1), jnp.concatenate(x_heads + st_heads, axis=0))
            upd = _dot(jnp.concatenate(ws, axis=1), jnp.concatenate(x_heads, axis=0))
            chunk_decay = jnp.where(lane_row < SSM_HEAD_DIM, decays[0], decays[1])
            st_ref[p] = stp * chunk_decay + upd
            if reverse:
                y_tot = y_pair + yf_ref[0, :, ps] + dskip_ref[:, ps] * xp
                outs.append(y_tot * _silu(z_ref[0, :, ps]))
            else:
                y_ref[0, :, ps] = y_pair
        if reverse:
            ss = outs[0] * outs[0]
            for u in outs[1:]:
                ss = ss + u * u
            inv = lax.rsqrt(jnp.sum(ss, axis=-1, keepdims=True) / (SSM_PAIRS_PER_GROUP * LANES) + RMS_EPS)
            for pp, u in enumerate(outs):
                ps = slice((g * SSM_PAIRS_PER_GROUP + pp) * LANES, (g * SSM_PAIRS_PER_GROUP + pp + 1) * LANES)
                y_ref[0, :, ps] = (u * inv * gn_ref[:, ps]).astype(y_ref.dtype)

    @pl.when(c == pl.num_programs(1) - 1)
    def _():
        sfin_ref[0] = st_ref[...]


def ssd_scan(zx, dtr, dt_bias, a_log, state0, reverse, yf=None, d_skip=None, gn_g=None):
    bsz, n, _ = zx.shape
    q = SSM_CHUNK
    nc = n // q
    cidx = (lambda c: nc - 1 - c) if reverse else (lambda c: c)
    b_col = 2 * SSM_INNER // SSM_GN
    state_spec = pl.BlockSpec((1, SSM_PAIRS, SSM_STATE, LANES), lambda b, c: (b, 0, 0, 0))
    in_specs = [pl.BlockSpec((1, q, SSM_INNER), lambda b, c: (b, cidx(c), 1)),
                pl.BlockSpec((1, q, SSM_GN), lambda b, c: (b, cidx(c), b_col)),
                pl.BlockSpec((1, q, SSM_GN), lambda b, c: (b, cidx(c), b_col + 1)),
                pl.BlockSpec((1, q, 2 * SSM_HEADS), lambda b, c: (b, cidx(c), 0)),
                pl.BlockSpec((1, 2 * SSM_HEADS), lambda b, c: (0, 0)),
                pl.BlockSpec((1, 2 * SSM_HEADS), lambda b, c: (0, 0)),
                state_spec]
    args = [zx, zx, zx, dtr, dt_bias.reshape(1, 2 * SSM_HEADS), a_log.reshape(1, 2 * SSM_HEADS), state0]
    if reverse:
        in_specs += [pl.BlockSpec((1, q, SSM_INNER), lambda b, c: (b, cidx(c), 0)),
                     pl.BlockSpec((1, q, SSM_INNER), lambda b, c: (b, cidx(c), 0)),
                     pl.BlockSpec((1, SSM_INNER), lambda b, c: (0, 0)),
                     pl.BlockSpec((1, SSM_INNER), lambda b, c: (0, 0))]
        args += [yf, zx, jnp.repeat(d_skip, SSM_HEAD_DIM).reshape(1, SSM_INNER), gn_g.reshape(1, SSM_INNER)]
    return pl.pallas_call(
        functools.partial(_ssd_kernel, reverse=reverse),
        grid=(bsz, nc),
        in_specs=in_specs,
        out_specs=[pl.BlockSpec((1, q, SSM_INNER), lambda b, c: (b, cidx(c), 0)), state_spec],
        out_shape=[jax.ShapeDtypeStruct((bsz, n, SSM_INNER), BF16 if reverse else F32),
                   jax.ShapeDtypeStruct((bsz, SSM_PAIRS, SSM_STATE, LANES), F32)],
        scratch_shapes=[pltpu.VMEM((SSM_PAIRS, SSM_STATE, LANES), F32)],
        compiler_params=_params("arbitrary", "arbitrary"),
        name="ssd_bwd" if reverse else "ssd_fwd",
    )(*args)


def _pool_kernel(prev_ref, h_ref, next_ref, mod_ref, g_ref, w_ref, sc_ref, o_ref, buf_ref, *, n_seq):
    t = pl.program_id(1)
    tr = h_ref.shape[1]
    halo = POOL_HALO
    g = g_ref[...]
    shift, scale = mod_ref[0, 3:4, :], mod_ref[0, 4:5, :]
    h = h_ref[0]
    buf_ref[0:halo, :] = jnp.where(t > 0, _rms_mod(prev_ref[0], g, shift, scale), 0.0)
    buf_ref[halo:halo + tr, :] = _rms_mod(h, g, shift, scale)
    buf_ref[halo + tr:, :] = jnp.where(t < pl.num_programs(1) - 1, _rms_mod(next_ref[0], g, shift, scale), 0.0)
    pos = t * tr + lax.broadcasted_iota(jnp.int32, (tr, 1), 0)
    for grp, win in enumerate(POOL_WINDOWS):
        cs = slice(grp * POOL_GROUP_DIM, (grp + 1) * POOL_GROUP_DIM)
        lo = halo - win // 2
        tot = buf_ref[lo:lo + tr, cs]
        for d in range(1, win):
            tot = tot + buf_ref[lo + d:lo + d + tr, cs]
        cnt = (jnp.minimum(pos - win // 2 + win, n_seq) - jnp.maximum(pos - win // 2, 0)).astype(F32)
        pooled = tot / cnt - buf_ref[halo:halo + tr, cs]
        y = _dot(pooled.astype(BF16), w_ref[grp]) * sc_ref[:, cs]
        o_ref[0, :, cs] = h[:, cs] + mod_ref[0, 5:6, cs] * y


def pool_mixer(h, mod, g, w_grp, scale):
    bsz, n, _ = h.shape
    tr = _tile(n, 256)
    halo = POOL_HALO
    last_halo = n // halo - 1
    per_batch = mod.shape[0] > 1
    mod_idx = (lambda b, t: (b, 0, 0)) if per_batch else (lambda b, t: (0, 0, 0))
    return pl.pallas_call(
        functools.partial(_pool_kernel, n_seq=n),
        grid=(bsz, n // tr),
        in_specs=[pl.BlockSpec((1, halo, D_MODEL), lambda b, t: (b, jnp.maximum(t * (tr // halo) - 1, 0), 0)),
                  pl.BlockSpec((1, tr, D_MODEL), lambda b, t: (b, t, 0)),
                  pl.BlockSpec((1, halo, D_MODEL),
                               lambda b, t: (b, jnp.minimum((t + 1) * (tr // halo), last_halo), 0)),
                  pl.BlockSpec((1, N_MOD, D_MODEL), mod_idx),
                  pl.BlockSpec((1, D_MODEL), lambda b, t: (0, 0)),
                  pl.BlockSpec((POOL_GROUPS, POOL_GROUP_DIM, POOL_GROUP_DIM), lambda b, t: (0, 0, 0)),
                  pl.BlockSpec((1, D_MODEL), lambda b, t: (0, 0))],
        out_specs=pl.BlockSpec((1, tr, D_MODEL), lambda b, t: (b, t, 0)),
        out_shape=jax.ShapeDtypeStruct((bsz, n, D_MODEL), F32),
        scratch_shapes=[pltpu.VMEM((tr + 2 * halo, D_MODEL), F32)],
        compiler_params=_params("arbitrary", "arbitrary"),
        name="pool_mixer",
    )(h, h, h, mod, g.reshape(1, D_MODEL), w_grp.astype(BF16), scale.reshape(1, D_MODEL))


def _attn_kernel(sink_ref, q_ref, kc_ref, vc_ref, *rest, band):
    if band:
        kp_ref, k0_ref, kn_ref, vp_ref, v0_ref, vn_ref, o_ref = rest
    else:
        (o_ref,) = rest
    blk = pl.program_id(1)
    nb = pl.num_programs(1)
    qb = ATT_BLOCK
    if band:
        shape = (Q_PER_KV * qb, 3 * qb)
        rows = lax.broadcasted_iota(jnp.int32, shape, 0) % qb
        cols = lax.broadcasted_iota(jnp.int32, shape, 1)
        diff = cols - rows
        ok = ((diff >= 0) & (diff <= 2 * WINDOW)
              & ((cols >= qb) | (blk > 0)) & ((cols < 2 * qb) | (blk < nb - 1)))
    for kvh in range(N_KV_HEADS):
        hs = slice(kvh * HEAD_DIM, (kvh + 1) * HEAD_DIM)
        q4 = jnp.concatenate([q_ref[0, :, (kvh * Q_PER_KV + g) * HEAD_DIM:(kvh * Q_PER_KV + g + 1) * HEAD_DIM]
                              for g in range(Q_PER_KV)], axis=0)
        sink = jnp.concatenate(
            [jnp.full((qb, 1), sink_ref[kvh * Q_PER_KV + g], F32) for g in range(Q_PER_KV)], axis=0)
        s_ctx = _dot_nt(q4, kc_ref[0, :, hs]) * ATT_SCALE
        m = jnp.maximum(jnp.max(s_ctx, axis=-1, keepdims=True), sink)
        if band:
            k_band = jnp.concatenate([kp_ref[0, :, hs], k0_ref[0, :, hs], kn_ref[0, :, hs]], axis=0)
            v_band = jnp.concatenate([vp_ref[0, :, hs], v0_ref[0, :, hs], vn_ref[0, :, hs]], axis=0)
            s_band = jnp.where(ok, _dot_nt(q4, k_band) * ATT_SCALE, -jnp.inf)
            m = jnp.maximum(m, jnp.max(s_band, axis=-1, keepdims=True))
        e_ctx = jnp.exp(s_ctx - m)
        denom = jnp.sum(e_ctx, axis=-1, keepdims=True) + jnp.exp(sink - m)
        if band:
            e_band = jnp.exp(s_band - m)
            denom = denom + jnp.sum(e_band, axis=-1, keepdims=True)
        inv = 1.0 / denom
        o4 = _dot((e_ctx * inv).astype(BF16), vc_ref[0, :, hs])
        if band:
            o4 = o4 + _dot((e_band * inv).astype(BF16), v_band)
        for g in range(Q_PER_KV):
            gs = slice((kvh * Q_PER_KV + g) * HEAD_DIM, (kvh * Q_PER_KV + g + 1) * HEAD_DIM)
            o_ref[0, :, gs] = o4[g * qb:(g + 1) * qb, :].astype(o_ref.dtype)


def attention(qkv, qkv_ctx, sink, band):
    bsz, n, _ = qkv.shape
    n_ctx = qkv_ctx.shape[1]
    qb = ATT_BLOCK
    nb = n // qb
    k_col = Q_DIM // KV_DIM
    v_col = k_col + 1
    in_specs = [pl.BlockSpec(memory_space=pltpu.SMEM),
                pl.BlockSpec((1, qb, Q_DIM), lambda b, i: (b, i, 0)),
                pl.BlockSpec((1, n_ctx, KV_DIM), lambda b, i: (b, 0, k_col)),
                pl.BlockSpec((1, n_ctx, KV_DIM), lambda b, i: (b, 0, v_col))]
    args = [sink, qkv, qkv_ctx, qkv_ctx]
    if band:
        for col in (k_col, v_col):
            in_specs += [
                pl.BlockSpec((1, qb, KV_DIM), lambda b, i, col=col: (b, jnp.maximum(i - 1, 0), col)),
                pl.BlockSpec((1, qb, KV_DIM), lambda b, i, col=col: (b, i, col)),
                pl.BlockSpec((1, qb, KV_DIM), lambda b, i, col=col: (b, jnp.minimum(i + 1, nb - 1), col))]
            args += [qkv, qkv, qkv]
    return pl.pallas_call(
        functools.partial(_attn_kernel, band=band),
        grid=(bsz, nb),
        in_specs=in_specs,
        out_specs=pl.BlockSpec((1, qb, Q_DIM), lambda b, i: (b, i, 0)),
        out_shape=jax.ShapeDtypeStruct((bsz, n, Q_DIM), BF16),
        compiler_params=_params("arbitrary", "arbitrary"),
        name="attn_band" if band else "attn_ctx",
    )(*args)


def _rope_tables(n_tokens):
    rows = n_tokens // GRID_W
    row = jnp.repeat(jnp.arange(rows, dtype=jnp.int32), GRID_W)
    col = jnp.tile(jnp.arange(GRID_W, dtype=jnp.int32), rows)
    inv_freq = ROPE_BASE ** (-jnp.arange(0, ROPE_AXIS_DIM, 2, dtype=F32) / ROPE_AXIS_DIM)
    ang = jnp.concatenate([row.astype(F32)[:, None] * inv_freq, col.astype(F32)[:, None] * inv_freq], axis=-1)
    cos = jnp.repeat(jnp.cos(ang), 2, axis=-1)
    sin = jnp.stack([-jnp.sin(ang), jnp.sin(ang)], axis=-1).reshape(n_tokens, HEAD_DIM)
    return cos, sin


def kernel(x, c, ctx, c_ctx, ada_w, ada_b, norm_g, ffn_w_in, ffn_w_out, ssm_w_in, ssm_conv_w, ssm_conv_b,
           ssm_dt_bias, ssm_a_log, ssm_d, ssm_norm_g, ssm_w_out, pool_w, pool_scale, attn_w_qkv, attn_sink,
           attn_w_o, final_g):
    bsz, n_lat, _ = x.shape
    n_ctx = ctx.shape[1]
    depth = ada_w.shape[0]
    n_mixers = 3
    cond_rows = 2 * SUBLANES
    cond = jnp.zeros((cond_rows, D_MODEL), F32).at[:bsz].set(c).at[bsz].set(c_ctx)
    mods = adaln_all(cond, ada_w, ada_b)
    cos, sin = _rope_tables(n_lat)

    h = x.reshape(bsz * n_lat, D_MODEL)
    hc = ctx.reshape(bsz * n_ctx, D_MODEL)
    for i in range(depth):
        kind, j = i % n_mixers, i // n_mixers
        last = i == depth - 1
        ctx_live = (not last) or kind != 1
        ml = mods[i, :bsz].reshape(bsz, N_MOD, D_MODEL)
        mc = mods[i, bsz:bsz + 1].reshape(1, N_MOD, D_MODEL)
        w_in = [ffn_w_in[i, s].astype(BF16) for s in range(2)]
        w_out = [ffn_w_out[i, s].astype(BF16) for s in range(2)]

        h = ffn(h, ml, norm_g[i, 0], w_in[0], w_out[0], 0, n_lat)
        if ctx_live:
            hc = ffn(hc, mc, norm_g[i, 0], w_in[0], w_out[0], 0, bsz * n_ctx)

        if kind == 0:
            w_zx = ssm_w_in[j][:, :SSM_ZX].astype(BF16)
            w_dt = ssm_w_in[j][:, SSM_ZX:].astype(BF16)
            w_o = ssm_w_out[j].astype(BF16)
            proj = (norm_g[i, 1], w_zx, w_dt, ssm_conv_w[j], ssm_conv_b[j])
            zx_l, dt_l = ssm_in_proj(h.reshape(bsz, n_lat, D_MODEL), ml, *proj)
            zx_c, dt_c = ssm_in_proj(hc.reshape(bsz, n_ctx, D_MODEL), mc, *proj)
            zero = jnp.zeros((bsz, SSM_PAIRS, SSM_STATE, LANES), F32)
            fin = dict(d_skip=ssm_d[j], gn_g=ssm_norm_g[j])
            yf_c, sf = ssd_scan(zx_c, dt_c, ssm_dt_bias[j], ssm_a_log[j], zero, False)
            yn_c, sb = ssd_scan(zx_c, dt_c, ssm_dt_bias[j], ssm_a_log[j], zero, True, yf=yf_c, **fin)
            yf_l, _ = ssd_scan(zx_l, dt_l, ssm_dt_bias[j], ssm_a_log[j], sf, False)
            yn_l, _ = ssd_scan(zx_l, dt_l, ssm_dt_bias[j], ssm_a_log[j], sb, True, yf=yf_l, **fin)
            h = out_proj(yn_l.reshape(bsz * n_lat, SSM_INNER), w_o, h, ml, n_lat)
            if not last:
                hc = out_proj(yn_c.reshape(bsz * n_ctx, SSM_INNER), w_o, hc, mc, bsz * n_ctx)
        elif kind == 1:
            h = pool_mixer(h.reshape(bsz, n_lat, D_MODEL), ml, norm_g[i, 1], pool_w[j],
                           pool_scale[j]).reshape(bsz * n_lat, D_MODEL)
            if not last:
                hc = pool_mixer(hc.reshape(bsz, n_ctx, D_MODEL), mc, norm_g[i, 1], pool_w[j],
                                pool_scale[j]).reshape(bsz * n_ctx, D_MODEL)
        else:
            w_qkv = attn_w_qkv[j].astype(BF16)
            w_o = attn_w_o[j].astype(BF16)
            tn = 512
            qkv_l = mod_proj(h, ml, norm_g[i, 1], w_qkv, 1, n_lat, tn, BF16,
                             rope=(cos, sin, Q_DIM + KV_DIM)).reshape(bsz, n_lat, QKV_DIM)
            qkv_c = mod_proj(hc, mc, norm_g[i, 1], w_qkv, 1, bsz * n_ctx, tn, BF16).reshape(bsz, n_ctx, QKV_DIM)
            o_l = attention(qkv_l, qkv_c, attn_sink[j], True)
            h = out_proj(o_l.reshape(bsz * n_lat, Q_DIM), w_o, h, ml, n_lat)
            if not last:
                o_c = attention(qkv_c, qkv_c, attn_sink[j], False)
                hc = out_proj(o_c.reshape(bsz * n_ctx, Q_DIM), w_o, hc, mc, bsz * n_ctx)

        h = ffn(h, ml, norm_g[i, 2], w_in[1], w_out[1], 2, n_lat, final_g=final_g if last else None)
        if not last:
            hc = ffn(hc, mc, norm_g[i, 2], w_in[1], w_out[1], 2, bsz * n_ctx)
    return h.reshape(bsz, n_lat, D_MODEL)
```

```python
import functools

import jax
import jax.numpy as jnp
from jax import lax
from jax.experimental import pallas as pl
from jax.experimental.pallas import tpu as pltpu

D_MODEL = 2048
N_MOD = 9
RMS_EPS = 1e-6
D_FF = 5632
GRID_W = 64

SSM_INNER = 2 * D_MODEL
SSM_HEAD_DIM = 64
SSM_HEADS = SSM_INNER // SSM_HEAD_DIM
SSM_GROUPS = 8
SSM_STATE = 128
SSM_CONV_W = 7
SSM_CHUNK = 128
SSM_GN = SSM_GROUPS * SSM_STATE
SSM_CONV_DIM = SSM_INNER + 2 * SSM_GN
SSM_PROJ = SSM_INNER + SSM_CONV_DIM + 2 * SSM_HEADS
SSM_PAIRS = SSM_HEADS // 2
SSM_PAIRS_PER_GROUP = SSM_PAIRS // SSM_GROUPS
SSM_ZX = SSM_INNER + SSM_CONV_DIM
SSM_IN_ROWS = 1024
SSM_IN_COLS = 1024
CONV_HALO = 16

POOL_WINDOWS = (2, 4, 8, 16)
POOL_GROUPS = 4
POOL_GROUP_DIM = D_MODEL // POOL_GROUPS
POOL_HALO = 8

HEAD_DIM = 128
N_HEADS = D_MODEL // HEAD_DIM
N_KV_HEADS = 4
Q_PER_KV = N_HEADS // N_KV_HEADS
Q_DIM = N_HEADS * HEAD_DIM
KV_DIM = N_KV_HEADS * HEAD_DIM
QKV_DIM = Q_DIM + 2 * KV_DIM
WINDOW = 128
ATT_BLOCK = 128
ATT_SCALE = HEAD_DIM ** -0.5
ROPE_BASE = 10000.0
ROPE_AXIS_DIM = HEAD_DIM // 2

LANES = 128
SUBLANES = 8
VMEM_LIMIT_BYTES = 56 * 1024 * 1024

F32 = jnp.float32
BF16 = jnp.bfloat16


def _params(*semantics):
    return pltpu.CompilerParams(dimension_semantics=semantics, vmem_limit_bytes=VMEM_LIMIT_BYTES)


def _tile(n, preferred):
    t = min(n, preferred)
    while n % t:
        t //= 2
    return t


def _dot(a, b):
    return jnp.dot(a, b, preferred_element_type=F32)


def _dot_nt(a, b):
    return lax.dot_general(a, b, (((1,), (1,)), ((), ())), preferred_element_type=F32)


def _sigmoid(x):
    return 1.0 / (1.0 + jnp.exp(-x))


def _silu(x):
    return x * _sigmoid(x)


def _rms_mod(h, g, shift, scale):
    ms = jnp.mean(h * h, axis=-1, keepdims=True)
    return (h * lax.rsqrt(ms + RMS_EPS)) * (g * (1.0 + scale)) + shift


def _adaln_kernel(c_ref, w_ref, b_ref, o_ref):
    s = _silu(c_ref[...]).astype(BF16)
    o_ref[0] = _dot(s, w_ref[0].astype(BF16)) + b_ref[0]


def adaln_all(cond, ada_w, ada_b):
    depth, _, n = ada_w.shape
    r = cond.shape[0]
    tn = _tile(n, 1024)
    return pl.pallas_call(
        _adaln_kernel,
        grid=(depth, n // tn),
        in_specs=[pl.BlockSpec((r, D_MODEL), lambda i, j: (0, 0)),
                  pl.BlockSpec((1, D_MODEL, tn), lambda i, j: (i, 0, j)),
                  pl.BlockSpec((1, 1, tn), lambda i, j: (i, 0, j))],
        out_specs=pl.BlockSpec((1, r, tn), lambda i, j: (i, 0, j)),
        out_shape=jax.ShapeDtypeStruct((depth, r, n), F32),
        compiler_params=_params("arbitrary", "arbitrary"),
        name="adaln",
    )(cond, ada_w, ada_b.reshape(depth, 1, n))


def _ffn_kernel(h_ref, mod_ref, g_ref, wg_ref, wu_ref, wo_ref, *rest, k, final_norm):
    if final_norm:
        fg_ref, o_ref, a_ref = rest
    else:
        o_ref, a_ref = rest
    j = pl.program_id(1)

    @pl.when(j == 0)
    def _():
        a_ref[...] = _rms_mod(h_ref[...], g_ref[...], mod_ref[0, 3 * k:3 * k + 1, :],
                              mod_ref[0, 3 * k + 1:3 * k + 2, :]).astype(BF16)
        o_ref[...] = jnp.zeros_like(o_ref)

    a = a_ref[...]
    act = (_silu(_dot(a, wg_ref[...])) * _dot(a, wu_ref[...])).astype(BF16)
    o_ref[...] += _dot(act, wo_ref[...])

    @pl.when(j == pl.num_programs(1) - 1)
    def _():
        h = h_ref[...] + (0.5 * mod_ref[0, 3 * k + 2:3 * k + 3, :]) * o_ref[...]
        if final_norm:
            h = h * lax.rsqrt(jnp.mean(h * h, axis=-1, keepdims=True) + RMS_EPS) * fg_ref[...]
        o_ref[...] = h


def ffn(h, mod, g, w_in, w_out, k, rows_per_mod, final_g=None):
    rows = h.shape[0]
    tm = _tile(rows_per_mod, 512)
    tf = _tile(D_FF, 512)
    nf = D_FF // tf
    per = rows_per_mod // tm
    vec_spec = pl.BlockSpec((1, D_MODEL), lambda i, j: (0, 0))
    in_specs = [pl.BlockSpec((tm, D_MODEL), lambda i, j: (i, 0)),
                pl.BlockSpec((1, N_MOD, D_MODEL), lambda i, j: (i // per, 0, 0)),
                vec_spec,
                pl.BlockSpec((D_MODEL, tf), lambda i, j: (0, j)),
                pl.BlockSpec((D_MODEL, tf), lambda i, j: (0, nf + j)),
                pl.BlockSpec((tf, D_MODEL), lambda i, j: (j, 0))]
    args = [h, mod, g.reshape(1, D_MODEL), w_in, w_in, w_out]
    if final_g is not None:
        in_specs.append(vec_spec)
        args.append(final_g.reshape(1, D_MODEL))
    return pl.pallas_call(
        functools.partial(_ffn_kernel, k=k, final_norm=final_g is not None),
        grid=(rows // tm, nf),
        in_specs=in_specs,
        out_specs=pl.BlockSpec((tm, D_MODEL), lambda i, j: (i, 0)),
        out_shape=jax.ShapeDtypeStruct((rows, D_MODEL), F32),
        scratch_shapes=[pltpu.VMEM((tm, D_MODEL), BF16)],
        compiler_params=_params("arbitrary", "arbitrary"),
        name="ffn",
    )(*args)


def _rope(x, cos, sin):
    lane = lax.broadcasted_iota(jnp.int32, x.shape, 1)
    partner = jnp.where(lane % 2 == 0, pltpu.roll(x, LANES - 1, 1), pltpu.roll(x, 1, 1))
    return x * cos + partner * sin


def _proj_kernel(h_ref, mod_ref, g_ref, w_ref, *rest, k, rope_tiles):
    if rope_tiles:
        cos_ref, sin_ref, o_ref, a_ref = rest
    else:
        o_ref, a_ref = rest
    j = pl.program_id(1)

    @pl.when(j == 0)
    def _():
        a_ref[...] = _rms_mod(h_ref[...], g_ref[...], mod_ref[0, 3 * k:3 * k + 1, :],
                              mod_ref[0, 3 * k + 1:3 * k + 2, :]).astype(BF16)

    res = _dot(a_ref[...], w_ref[...])
    if not rope_tiles:
        o_ref[...] = res.astype(o_ref.dtype)
        return

    @pl.when(j < rope_tiles)
    def _():
        cos, sin = cos_ref[...], sin_ref[...]
        for c in range(res.shape[1] // HEAD_DIM):
            sl = slice(c * HEAD_DIM, (c + 1) * HEAD_DIM)
            o_ref[:, sl] = _rope(res[:, sl], cos, sin).astype(o_ref.dtype)

    @pl.when(j >= rope_tiles)
    def _():
        o_ref[...] = res.astype(o_ref.dtype)


def mod_proj(h, mod, g, w, k, rows_per_mod, tn, out_dtype, rope=None):
    rows = h.shape[0]
    n = w.shape[1]
    tm = _tile(rows_per_mod, 1024)
    per = rows_per_mod // tm
    in_specs = [pl.BlockSpec((tm, D_MODEL), lambda i, j: (i, 0)),
                pl.BlockSpec((1, N_MOD, D_MODEL), lambda i, j: (i // per, 0, 0)),
                pl.BlockSpec((1, D_MODEL), lambda i, j: (0, 0)),
                pl.BlockSpec((D_MODEL, tn), lambda i, j: (0, j))]
    args = [h, mod, g.reshape(1, D_MODEL), w]
    rope_tiles = 0
    if rope is not None:
        cos, sin, n_cols = rope
        rope_tiles = n_cols // tn
        in_specs += [pl.BlockSpec((tm, HEAD_DIM), lambda i, j: (i % per, 0))] * 2
        args += [cos, sin]
    return pl.pallas_call(
        functools.partial(_proj_kernel, k=k, rope_tiles=rope_tiles),
        grid=(rows // tm, n // tn),
        in_specs=in_specs,
        out_specs=pl.BlockSpec((tm, tn), lambda i, j: (i, j)),
        out_shape=jax.ShapeDtypeStruct((rows, n), out_dtype),
        scratch_shapes=[pltpu.VMEM((tm, D_MODEL), BF16)],
        compiler_params=_params("arbitrary", "arbitrary"),
        name="mod_proj",
    )(*args)


def _out_proj_kernel(x_ref, w_ref, h_ref, mod_ref, o_ref):
    o_ref[...] = h_ref[...] + mod_ref[0, 5:6, :] * _dot(x_ref[...], w_ref[...])


def out_proj(x, w, h, mod, rows_per_mod):
    rows, kdim = x.shape
    tm = _tile(rows_per_mod, 512)
    tn = _tile(D_MODEL, 1024)
    per = rows_per_mod // tm
    return pl.pallas_call(
        _out_proj_kernel,
        grid=(D_MODEL // tn, rows // tm),
        in_specs=[pl.BlockSpec((tm, kdim), lambda j, i: (i, 0)),
                  pl.BlockSpec((kdim, tn), lambda j, i: (0, j)),
                  pl.BlockSpec((tm, tn), lambda j, i: (i, j)),
                  pl.BlockSpec((1, N_MOD, tn), lambda j, i: (i // per, 0, j))],
        out_specs=pl.BlockSpec((tm, tn), lambda j, i: (i, j)),
        out_shape=jax.ShapeDtypeStruct((rows, D_MODEL), F32),
        compiler_params=_params("arbitrary", "arbitrary"),
        name="out_proj",
    )(x, w, h, mod)


def _ssm_in_kernel(prev_ref, h_ref, next_ref, mod_ref, g_ref, w_ref, wdt_ref, cw_ref, cb_ref,
                   o_ref, dt_ref, a_ref, res_ref, *, z_tiles):
    t = pl.program_id(1)
    j = pl.program_id(2)
    tm = h_ref.shape[1]
    halo = CONV_HALO
    groups = (tm + 2 * halo) // SUBLANES
    sub = lax.broadcasted_iota(jnp.int32, (1, SUBLANES, 1), 1)

    def shift_rows(x, up):
        if up:
            rot = pltpu.roll(x, SUBLANES - 1, 1)
            return jnp.where(sub == SUBLANES - 1, jnp.concatenate([rot[1:], rot[:1]], axis=0), rot)
        rot = pltpu.roll(x, 1, 1)
        return jnp.where(sub == 0, jnp.concatenate([rot[-1:], rot[:-1]], axis=0), rot)

    @pl.when(j == 0)
    def _():
        g, shift, scale = g_ref[...], mod_ref[0, 3:4, :], mod_ref[0, 4:5, :]
        a_ref[0:halo, :] = jnp.where(t > 0, _rms_mod(prev_ref[0], g, shift, scale), 0.0).astype(BF16)
        a_ref[halo:halo + tm, :] = _rms_mod(h_ref[0], g, shift, scale).astype(BF16)
        a_ref[halo + tm:, :] = jnp.where(t < pl.num_programs(1) - 1,
                                         _rms_mod(next_ref[0], g, shift, scale), 0.0).astype(BF16)
        dt_ref[0] = _dot(a_ref[halo:halo + tm, :], wdt_ref[...])

    @pl.when(j < z_tiles)
    def _():
        o_ref[0] = _dot(a_ref[halo:halo + tm, :], w_ref[...])

    @pl.when(j >= z_tiles)
    def _():
        res_ref[...] = _dot(a_ref[...], w_ref[...])
        for c in range(o_ref.shape[2] // LANES):
            cs = slice(c * LANES, (c + 1) * LANES)
            x = res_ref[:, cs].reshape(groups, SUBLANES, LANES)
            w = cw_ref[:, cs].reshape(SSM_CONV_W, 1, LANES)
            up = w[6:7] * x
            up = shift_rows(up, True) + w[5:6] * x
            up = shift_rows(up, True) + w[4:5] * x
            down = w[0:1] * x
            down = shift_rows(down, False) + w[1:2] * x
            down = shift_rows(down, False) + w[2:3] * x
            out = w[3:4] * x + shift_rows(up, True) + shift_rows(down, False) + cb_ref[:, cs].reshape(1, 1, LANES)
            first = halo // SUBLANES
            o_ref[0, :, cs] = _silu(out[first:first + tm // SUBLANES]).reshape(tm, LANES)


def ssm_in_proj(h, mod, g, w_zx, w_dt, conv_w, conv_b):
    bsz, n, _ = h.shape
    tm = _tile(n, SSM_IN_ROWS)
    tn = SSM_IN_COLS
    halo = CONV_HALO
    z_tiles = SSM_INNER // tn
    last_halo = n // halo - 1
    per_batch = mod.shape[0] > 1
    mod_idx = (lambda b, t, j: (b, 0, 0)) if per_batch else (lambda b, t, j: (0, 0, 0))
    conv_idx = lambda b, t, j: (0, jnp.maximum(j - z_tiles, 0))
    return pl.pallas_call(
        functools.partial(_ssm_in_kernel, z_tiles=z_tiles),
        grid=(bsz, n // tm, SSM_ZX // tn),
        in_specs=[pl.BlockSpec((1, halo, D_MODEL), lambda b, t, j: (b, jnp.maximum(t * (tm // halo) - 1, 0), 0)),
                  pl.BlockSpec((1, tm, D_MODEL), lambda b, t, j: (b, t, 0)),
                  pl.BlockSpec((1, halo, D_MODEL),
                               lambda b, t, j: (b, jnp.minimum((t + 1) * (tm // halo), last_halo), 0)),
                  pl.BlockSpec((1, N_MOD, D_MODEL), mod_idx),
                  pl.BlockSpec((1, D_MODEL), lambda b, t, j: (0, 0)),
                  pl.BlockSpec((D_MODEL, tn), lambda b, t, j: (0, j)),
                  pl.BlockSpec((D_MODEL, 2 * SSM_HEADS), lambda b, t, j: (0, 0)),
                  pl.BlockSpec((SSM_CONV_W, tn), conv_idx),
                  pl.BlockSpec((1, tn), conv_idx)],
        out_specs=[pl.BlockSpec((1, tm, tn), lambda b, t, j: (b, t, j)),
                   pl.BlockSpec((1, tm, 2 * SSM_HEADS), lambda b, t, j: (b, t, 0))],
        out_shape=[jax.ShapeDtypeStruct((bsz, n, SSM_ZX), F32),
                   jax.ShapeDtypeStruct((bsz, n, 2 * SSM_HEADS), F32)],
        scratch_shapes=[pltpu.VMEM((tm + 2 * halo, D_MODEL), BF16),
                        pltpu.VMEM((tm + 2 * halo, tn), F32)],
        compiler_params=_params("arbitrary", "arbitrary", "arbitrary"),
        name="ssm_in_proj",
    )(h, h, h, mod, g.reshape(1, D_MODEL), w_zx, w_dt, conv_w, conv_b.reshape(1, SSM_CONV_DIM))


def _ssd_kernel(x_ref, b_ref, c_ref, dtr_ref, dtb_ref, alog_ref, s0_ref, *rest, reverse):
    if reverse:
        yf_ref, z_ref, dskip_ref, gn_ref, y_ref, sfin_ref, st_ref = rest
    else:
        y_ref, sfin_ref, st_ref = rest
    c = pl.program_id(1)
    q = SSM_CHUNK

    @pl.when(c == 0)
    def _():
        st_ref[...] = s0_ref[0]

    pre = dtr_ref[0] + dtb_ref[...]
    dt = jnp.maximum(pre, 0.0) + jnp.log(1.0 + jnp.exp(-jnp.abs(pre)))
    da = dt * (-jnp.exp(alog_ref[...]))
    row = lax.broadcasted_iota(jnp.int32, (q, q), 0)
    col = lax.broadcasted_iota(jnp.int32, (q, q), 1)
    tri = (col >= row) if reverse else (col <= row)
    tri_bf = jnp.where(tri, 1.0, 0.0).astype(BF16)
    hi = da.astype(BF16)
    r1 = da - hi.astype(F32)
    mid = r1.astype(BF16)
    lo = (r1 - mid.astype(F32)).astype(BF16)
    cum = _dot(tri_bf, hi) + _dot(tri_bf, mid) + _dot(tri_bf, lo)
    cum_t = cum.T
    src_t = cum_t - jnp.log(dt.T)
    head0 = SSM_HEADS if reverse else 0
    last = 0 if reverse else q - 1
    is_a = col < SSM_HEAD_DIM
    lane_row = lax.broadcasted_iota(jnp.int32, (1, LANES), 1)

    for g in range(SSM_GROUPS):
        gs = slice(g * SSM_STATE, (g + 1) * SSM_STATE)
        bg = b_ref[0, :, gs]
        cg = c_ref[0, :, gs]
        bg_t = bg.T
        cg_bf = cg.astype(BF16)
        cb = _dot_nt(cg_bf, bg.astype(BF16))
        outs = []
        for pp in range(SSM_PAIRS_PER_GROUP):
            p = g * SSM_PAIRS_PER_GROUP + pp
            ps = slice(p * LANES, (p + 1) * LANES)
            xp = x_ref[0, :, ps]
            stp = st_ref[p]
            x_heads = [jnp.where(is_a, xp, 0.0).astype(BF16), jnp.where(is_a, 0.0, xp).astype(BF16)]
            ms, ws, decays, cum_cols = [], [], [], []
            for hh in range(2):
                hc = head0 + 2 * p + hh
                cum_col = jnp.broadcast_to(cum[:, hc:hc + 1], (q, q))
                src_row = src_t[hc:hc + 1, :]
                decay_dt = jnp.exp(jnp.where(tri, cum_col - src_row, -jnp.inf))
                ms.append((cb * decay_dt).astype(BF16))
                cum_last = cum_t[hc:hc + 1, last:last + 1]
                w_row = jnp.exp(cum_last - src_row)
                ws.append((bg_t * w_row).astype(BF16))
                decays.append(jnp.exp(cum_last))
                cum_cols.append(cum_col)
            y_state = jnp.exp(jnp.where(is_a, cum_cols[0], cum_cols[1])) * _dot(cg_bf, stp.astype(BF16))
            y_pair = _dot(jnp.concatenate(ms, axis=1), jnp.concatenate(x_heads, axis=0)) + y_state
            upd = _dot(jnp.concatenate(ws, axis=1), jnp.concatenate(x_heads, axis=0))
            chunk_decay = jnp.where(lane_row < SSM_HEAD_DIM, decays[0], decays[1])
            st_ref[p] = stp * chunk_decay + upd
            if reverse:
                y_tot = y_pair + yf_ref[0, :, ps] + dskip_ref[:, ps] * xp
                outs.append(y_tot * _silu(z_ref[0, :, ps]))
            else:
                y_ref[0, :, ps] = y_pair
        if reverse:
            ss = outs[0] * outs[0]
            for u in outs[1:]:
                ss = ss + u * u
            inv = lax.rsqrt(jnp.sum(ss, axis=-1, keepdims=True) / (SSM_PAIRS_PER_GROUP * LANES) + RMS_EPS)
            for pp, u in enumerate(outs):
                ps = slice((g * SSM_PAIRS_PER_GROUP + pp) * LANES, (g * SSM_PAIRS_PER_GROUP + pp + 1) * LANES)
                y_ref[0, :, ps] = (u * inv * gn_ref[:, ps]).astype(y_ref.dtype)

    @pl.when(c == pl.num_programs(1) - 1)
    def _():
        sfin_ref[0] = st_ref[...]


def ssd_scan(zx, dtr, dt_bias, a_log, state0, reverse, yf=None, d_skip=None, gn_g=None):
    bsz, n, _ = zx.shape
    q = SSM_CHUNK
    nc = n // q
    cidx = (lambda c: nc - 1 - c) if reverse else (lambda c: c)
    b_col = 2 * SSM_INNER // SSM_GN
    state_spec = pl.BlockSpec((1, SSM_PAIRS, SSM_STATE, LANES), lambda b, c: (b, 0, 0, 0))
    in_specs = [pl.BlockSpec((1, q, SSM_INNER), lambda b, c: (b, cidx(c), 1)),
                pl.BlockSpec((1, q, SSM_GN), lambda b, c: (b, cidx(c), b_col)),
                pl.BlockSpec((1, q, SSM_GN), lambda b, c: (b, cidx(c), b_col + 1)),
                pl.BlockSpec((1, q, 2 * SSM_HEADS), lambda b, c: (b, cidx(c), 0)),
                pl.BlockSpec((1, 2 * SSM_HEADS), lambda b, c: (0, 0)),
                pl.BlockSpec((1, 2 * SSM_HEADS), lambda b, c: (0, 0)),
                state_spec]
    args = [zx, zx, zx, dtr, dt_bias.reshape(1, 2 * SSM_HEADS), a_log.reshape(1, 2 * SSM_HEADS), state0]
    if reverse:
        in_specs += [pl.BlockSpec((1, q, SSM_INNER), lambda b, c: (b, cidx(c), 0)),
                     pl.BlockSpec((1, q, SSM_INNER), lambda b, c: (b, cidx(c), 0)),
                     pl.BlockSpec((1, SSM_INNER), lambda b, c: (0, 0)),
                     pl.BlockSpec((1, SSM_INNER), lambda b, c: (0, 0))]
        args += [yf, zx, jnp.repeat(d_skip, SSM_HEAD_DIM).reshape(1, SSM_INNER), gn_g.reshape(1, SSM_INNER)]
    return pl.pallas_call(
        functools.partial(_ssd_kernel, reverse=reverse),
        grid=(bsz, nc),
        in_specs=in_specs,
        out_specs=[pl.BlockSpec((1, q, SSM_INNER), lambda b, c: (b, cidx(c), 0)), state_spec],
        out_shape=[jax.ShapeDtypeStruct((bsz, n, SSM_INNER), BF16 if reverse else F32),
                   jax.ShapeDtypeStruct((bsz, SSM_PAIRS, SSM_STATE, LANES), F32)],
        scratch_shapes=[pltpu.VMEM((SSM_PAIRS, SSM_STATE, LANES), F32)],
        compiler_params=_params("arbitrary", "arbitrary"),
        name="ssd_bwd" if reverse else "ssd_fwd",
    )(*args)


def _pool_kernel(prev_ref, h_ref, next_ref, mod_ref, g_ref, w_ref, sc_ref, o_ref, buf_ref, *, n_seq):
    t = pl.program_id(1)
    tr = h_ref.shape[1]
    halo = POOL_HALO
    g = g_ref[...]
    shift, scale = mod_ref[0, 3:4, :], mod_ref[0, 4:5, :]
    h = h_ref[0]
    buf_ref[0:halo, :] = jnp.where(t > 0, _rms_mod(prev_ref[0], g, shift, scale), 0.0)
    buf_ref[halo:halo + tr, :] = _rms_mod(h, g, shift, scale)
    buf_ref[halo + tr:, :] = jnp.where(t < pl.num_programs(1) - 1, _rms_mod(next_ref[0], g, shift, scale), 0.0)
    pos = t * tr + lax.broadcasted_iota(jnp.int32, (tr, 1), 0)
    for grp, win in enumerate(POOL_WINDOWS):
        cs = slice(grp * POOL_GROUP_DIM, (grp + 1) * POOL_GROUP_DIM)
        lo = halo - win // 2
        tot = buf_ref[lo:lo + tr, cs]
        for d in range(1, win):
            tot = tot + buf_ref[lo + d:lo + d + tr, cs]
        cnt = (jnp.minimum(pos - win // 2 + win, n_seq) - jnp.maximum(pos - win // 2, 0)).astype(F32)
        pooled = tot / cnt - buf_ref[halo:halo + tr, cs]
        y = _dot(pooled.astype(BF16), w_ref[grp]) * sc_ref[:, cs]
        o_ref[0, :, cs] = h[:, cs] + mod_ref[0, 5:6, cs] * y


def pool_mixer(h, mod, g, w_grp, scale):
    bsz, n, _ = h.shape
    tr = _tile(n, 256)
    halo = POOL_HALO
    last_halo = n // halo - 1
    per_batch = mod.shape[0] > 1
    mod_idx = (lambda b, t: (b, 0, 0)) if per_batch else (lambda b, t: (0, 0, 0))
    return pl.pallas_call(
        functools.partial(_pool_kernel, n_seq=n),
        grid=(bsz, n // tr),
        in_specs=[pl.BlockSpec((1, halo, D_MODEL), lambda b, t: (b, jnp.maximum(t * (tr // halo) - 1, 0), 0)),
                  pl.BlockSpec((1, tr, D_MODEL), lambda b, t: (b, t, 0)),
                  pl.BlockSpec((1, halo, D_MODEL),
                               lambda b, t: (b, jnp.minimum((t + 1) * (tr // halo), last_halo), 0)),
                  pl.BlockSpec((1, N_MOD, D_MODEL), mod_idx),
                  pl.BlockSpec((1, D_MODEL), lambda b, t: (0, 0)),
                  pl.BlockSpec((POOL_GROUPS, POOL_GROUP_DIM, POOL_GROUP_DIM), lambda b, t: (0, 0, 0)),
                  pl.BlockSpec((1, D_MODEL), lambda b, t: (0, 0))],
        out_specs=pl.BlockSpec((1, tr, D_MODEL), lambda b, t: (b, t, 0)),
        out_shape=jax.ShapeDtypeStruct((bsz, n, D_MODEL), F32),
        scratch_shapes=[pltpu.VMEM((tr + 2 * halo, D_MODEL), F32)],
        compiler_params=_params("arbitrary", "arbitrary"),
        name="pool_mixer",
    )(h, h, h, mod, g.reshape(1, D_MODEL), w_grp.astype(BF16), scale.reshape(1, D_MODEL))


def _attn_kernel(sink_ref, q_ref, kc_ref, vc_ref, *rest, band):
    if band:
        kp_ref, k0_ref, kn_ref, vp_ref, v0_ref, vn_ref, o_ref = rest
    else:
        (o_ref,) = rest
    blk = pl.program_id(1)
    nb = pl.num_programs(1)
    qb = ATT_BLOCK
    if band:
        shape = (Q_PER_KV * qb, 3 * qb)
        rows = lax.broadcasted_iota(jnp.int32, shape, 0) % qb
        cols = lax.broadcasted_iota(jnp.int32, shape, 1)
        diff = cols - rows
        ok = ((diff >= 0) & (diff <= 2 * WINDOW)
              & ((cols >= qb) | (blk > 0)) & ((cols < 2 * qb) | (blk < nb - 1)))
    for kvh in range(N_KV_HEADS):
        hs = slice(kvh * HEAD_DIM, (kvh + 1) * HEAD_DIM)
        q4 = jnp.concatenate([q_ref[0, :, (kvh * Q_PER_KV + g) * HEAD_DIM:(kvh * Q_PER_KV + g + 1) * HEAD_DIM]
                              for g in range(Q_PER_KV)], axis=0)
        sink = jnp.concatenate(
            [jnp.full((qb, 1), sink_ref[kvh * Q_PER_KV + g], F32) for g in range(Q_PER_KV)], axis=0)
        s_ctx = _dot_nt(q4, kc_ref[0, :, hs]) * ATT_SCALE
        m = jnp.maximum(jnp.max(s_ctx, axis=-1, keepdims=True), sink)
        if band:
            k_band = jnp.concatenate([kp_ref[0, :, hs], k0_ref[0, :, hs], kn_ref[0, :, hs]], axis=0)
            v_band = jnp.concatenate([vp_ref[0, :, hs], v0_ref[0, :, hs], vn_ref[0, :, hs]], axis=0)
            s_band = jnp.where(ok, _dot_nt(q4, k_band) * ATT_SCALE, -jnp.inf)
            m = jnp.maximum(m, jnp.max(s_band, axis=-1, keepdims=True))
        ones = jnp.ones((kc_ref.shape[1], HEAD_DIM), BF16)
        acc = _dot(jnp.exp(s_ctx - m).astype(BF16), jnp.concatenate([vc_ref[0, :, hs], ones], axis=1))
        if band:
            ones = jnp.ones((3 * qb, HEAD_DIM), BF16)
            acc = acc + _dot(jnp.exp(s_band - m).astype(BF16), jnp.concatenate([v_band, ones], axis=1))
        o4 = acc[:, :HEAD_DIM] / (acc[:, HEAD_DIM:] + jnp.exp(sink - m))
        for g in range(Q_PER_KV):
            gs = slice((kvh * Q_PER_KV + g) * HEAD_DIM, (kvh * Q_PER_KV + g + 1) * HEAD_DIM)
            o_ref[0, :, gs] = o4[g * qb:(g + 1) * qb, :].astype(o_ref.dtype)


def attention(qkv, qkv_ctx, sink, band):
    bsz, n, _ = qkv.shape
    n_ctx = qkv_ctx.shape[1]
    qb = ATT_BLOCK
    nb = n // qb
    k_col = Q_DIM // KV_DIM
    v_col = k_col + 1
    in_specs = [pl.BlockSpec(memory_space=pltpu.SMEM),
                pl.BlockSpec((1, qb, Q_DIM), lambda b, i: (b, i, 0)),
                pl.BlockSpec((1, n_ctx, KV_DIM), lambda b, i: (b, 0, k_col)),
                pl.BlockSpec((1, n_ctx, KV_DIM), lambda b, i: (b, 0, v_col))]
    args = [sink, qkv, qkv_ctx, qkv_ctx]
    if band:
        for col in (k_col, v_col):
            in_specs += [
                pl.BlockSpec((1, qb, KV_DIM), lambda b, i, col=col: (b, jnp.maximum(i - 1, 0), col)),
                pl.BlockSpec((1, qb, KV_DIM), lambda b, i, col=col: (b, i, col)),
                pl.BlockSpec((1, qb, KV_DIM), lambda b, i, col=col: (b, jnp.minimum(i + 1, nb - 1), col))]
            args += [qkv, qkv, qkv]
    return pl.pallas_call(
        functools.partial(_attn_kernel, band=band),
        grid=(bsz, nb),
        in_specs=in_specs,
        out_specs=pl.BlockSpec((1, qb, Q_DIM), lambda b, i: (b, i, 0)),
        out_shape=jax.ShapeDtypeStruct((bsz, n, Q_DIM), BF16),
        compiler_params=_params("arbitrary", "arbitrary"),
        name="attn_band" if band else "attn_ctx",
    )(*args)


def _rope_tables(n_tokens):
    rows = n_tokens // GRID_W
    row = jnp.repeat(jnp.arange(rows, dtype=jnp.int32), GRID_W)
    col = jnp.tile(jnp.arange(GRID_W, dtype=jnp.int32), rows)
    inv_freq = ROPE_BASE ** (-jnp.arange(0, ROPE_AXIS_DIM, 2, dtype=F32) / ROPE_AXIS_DIM)
    ang = jnp.concatenate([row.astype(F32)[:, None] * inv_freq, col.astype(F32)[:, None] * inv_freq], axis=-1)
    cos = jnp.repeat(jnp.cos(ang), 2, axis=-1)
    sin = jnp.stack([-jnp.sin(ang), jnp.sin(ang)], axis=-1).reshape(n_tokens, HEAD_DIM)
    return cos, sin


def kernel(x, c, ctx, c_ctx, ada_w, ada_b, norm_g, ffn_w_in, ffn_w_out, ssm_w_in, ssm_conv_w, ssm_conv_b,
           ssm_dt_bias, ssm_a_log, ssm_d, ssm_norm_g, ssm_w_out, pool_w, pool_scale, attn_w_qkv, attn_sink,
           attn_w_o, final_g):
    bsz, n_lat, _ = x.shape
    n_ctx = ctx.shape[1]
    depth = ada_w.shape[0]
    n_mixers = 3
    cond_rows = 2 * SUBLANES
    cond = jnp.zeros((cond_rows, D_MODEL), F32).at[:bsz].set(c).at[bsz].set(c_ctx)
    mods = adaln_all(cond, ada_w, ada_b)
    cos, sin = _rope_tables(n_lat)

    h = x.reshape(bsz * n_lat, D_MODEL)
    hc = ctx.reshape(bsz * n_ctx, D_MODEL)
    for i in range(depth):
        kind, j = i % n_mixers, i // n_mixers
        last = i == depth - 1
        ctx_live = (not last) or kind != 1
        ml = mods[i, :bsz].reshape(bsz, N_MOD, D_MODEL)
        mc = mods[i, bsz:bsz + 1].reshape(1, N_MOD, D_MODEL)
        w_in = [ffn_w_in[i, s].astype(BF16) for s in range(2)]
        w_out = [ffn_w_out[i, s].astype(BF16) for s in range(2)]

        h = ffn(h, ml, norm_g[i, 0], w_in[0], w_out[0], 0, n_lat)
        if ctx_live:
            hc = ffn(hc, mc, norm_g[i, 0], w_in[0], w_out[0], 0, bsz * n_ctx)

        if kind == 0:
            w_zx = ssm_w_in[j][:, :SSM_ZX].astype(BF16)
            w_dt = ssm_w_in[j][:, SSM_ZX:].astype(BF16)
            w_o = ssm_w_out[j].astype(BF16)
            proj = (norm_g[i, 1], w_zx, w_dt, ssm_conv_w[j], ssm_conv_b[j])
            zx_l, dt_l = ssm_in_proj(h.reshape(bsz, n_lat, D_MODEL), ml, *proj)
            zx_c, dt_c = ssm_in_proj(hc.reshape(bsz, n_ctx, D_MODEL), mc, *proj)
            zero = jnp.zeros((bsz, SSM_PAIRS, SSM_STATE, LANES), F32)
            fin = dict(d_skip=ssm_d[j], gn_g=ssm_norm_g[j])
            yf_c, sf = ssd_scan(zx_c, dt_c, ssm_dt_bias[j], ssm_a_log[j], zero, False)
            yn_c, sb = ssd_scan(zx_c, dt_c, ssm_dt_bias[j], ssm_a_log[j], zero, True, yf=yf_c, **fin)
            yf_l, _ = ssd_scan(zx_l, dt_l, ssm_dt_bias[j], ssm_a_log[j], sf, False)
            yn_l, _ = ssd_scan(zx_l, dt_l, ssm_dt_bias[j], ssm_a_log[j], sb, True, yf=yf_l, **fin)
            h = out_proj(yn_l.reshape(bsz * n_lat, SSM_INNER), w_o, h, ml, n_lat)
            if not last:
                hc = out_proj(yn_c.reshape(bsz * n_ctx, SSM_INNER), w_o, hc, mc, bsz * n_ctx)
        elif kind == 1:
            h = pool_mixer(h.reshape(bsz, n_lat, D_MODEL), ml, norm_g[i, 1], pool_w[j],
                           pool_scale[j]).reshape(bsz * n_lat, D_MODEL)
            if not last:
                hc = pool_mixer(hc.reshape(bsz, n_ctx, D_MODEL), mc, norm_g[i, 1], pool_w[j],
                                pool_scale[j]).reshape(bsz * n_ctx, D_MODEL)
        else:
            w_qkv = attn_w_qkv[j].astype(BF16)
            w_o = attn_w_o[j].astype(BF16)
            tn = 512
            qkv_l = mod_proj(h, ml, norm_g[i, 1], w_qkv, 1, n_lat, tn, BF16,
                             rope=(cos, sin, Q_DIM + KV_DIM)).reshape(bsz, n_lat, QKV_DIM)
            qkv_c = mod_proj(hc, mc, norm_g[i, 1], w_qkv, 1, bsz * n_ctx, tn, BF16).reshape(bsz, n_ctx, QKV_DIM)
            o_l = attention(qkv_l, qkv_c, attn_sink[j], True)
            h = out_proj(o_l.reshape(bsz * n_lat, Q_DIM), w_o, h, ml, n_lat)
            if not last:
                o_c = attention(qkv_c, qkv_c, attn_sink[j], False)
                hc = out_proj(o_c.reshape(bsz * n_ctx, Q_DIM), w_o, hc, mc, bsz * n_ctx)

        h = ffn(h, ml, norm_g[i, 2], w_in[1], w_out[1], 2, n_lat, final_g=final_g if last else None)
        if not last:
            hc = ffn(hc, mc, norm_g[i, 2], w_in[1], w_out[1], 2, bsz * n_ctx)
    return h.reshape(bsz, n_lat, D_MODEL)
```

```python
import functools

import jax
import jax.numpy as jnp
from jax import lax
from jax.experimental import pallas as pl
from jax.experimental.pallas import tpu as pltpu

D_MODEL = 2048
N_MOD = 9
RMS_EPS = 1e-6
D_FF = 5632
GRID_W = 64

SSM_INNER = 2 * D_MODEL
SSM_HEAD_DIM = 64
SSM_HEADS = SSM_INNER // SSM_HEAD_DIM
SSM_GROUPS = 8
SSM_STATE = 128
SSM_CONV_W = 7
SSM_CHUNK = 128
SSM_GN = SSM_GROUPS * SSM_STATE
SSM_CONV_DIM = SSM_INNER + 2 * SSM_GN
SSM_PROJ = SSM_INNER + SSM_CONV_DIM + 2 * SSM_HEADS
SSM_PAIRS = SSM_HEADS // 2
SSM_PAIRS_PER_GROUP = SSM_PAIRS // SSM_GROUPS
SSM_ZX = SSM_INNER + SSM_CONV_DIM
SSM_IN_ROWS = 1024
SSM_IN_COLS = 1024
CONV_HALO = 16
FFN_ROWS = 512
NORM_ROWS = 32

POOL_WINDOWS = (2, 4, 8, 16)
POOL_GROUPS = 4
POOL_GROUP_DIM = D_MODEL // POOL_GROUPS
POOL_HALO = 8

HEAD_DIM = 128
N_HEADS = D_MODEL // HEAD_DIM
N_KV_HEADS = 4
Q_PER_KV = N_HEADS // N_KV_HEADS
Q_DIM = N_HEADS * HEAD_DIM
KV_DIM = N_KV_HEADS * HEAD_DIM
QKV_DIM = Q_DIM + 2 * KV_DIM
WINDOW = 128
ATT_BLOCK = 128
ATT_SCALE = HEAD_DIM ** -0.5
ROPE_BASE = 10000.0
ROPE_AXIS_DIM = HEAD_DIM // 2

LANES = 128
SUBLANES = 8
VMEM_LIMIT_BYTES = 56 * 1024 * 1024

F32 = jnp.float32
BF16 = jnp.bfloat16


def _params(*semantics):
    return pltpu.CompilerParams(dimension_semantics=semantics, vmem_limit_bytes=VMEM_LIMIT_BYTES)


def _tile(n, preferred):
    t = min(n, preferred)
    while n % t:
        t //= 2
    return t


def _dot(a, b):
    return jnp.dot(a, b, preferred_element_type=F32)


def _dot_nt(a, b):
    return lax.dot_general(a, b, (((1,), (1,)), ((), ())), preferred_element_type=F32)


def _sigmoid(x):
    return 1.0 / (1.0 + jnp.exp(-x))


def _silu(x):
    return x * _sigmoid(x)


def _rms_mod(h, g, shift, scale):
    ms = jnp.mean(h * h, axis=-1, keepdims=True)
    return (h * lax.rsqrt(ms + RMS_EPS)) * (g * (1.0 + scale)) + shift


def _adaln_kernel(c_ref, w_ref, b_ref, o_ref):
    s = _silu(c_ref[...]).astype(BF16)
    o_ref[0] = _dot(s, w_ref[0].astype(BF16)) + b_ref[0]


def adaln_all(cond, ada_w, ada_b):
    depth, _, n = ada_w.shape
    r = cond.shape[0]
    tn = _tile(n, 1024)
    return pl.pallas_call(
        _adaln_kernel,
        grid=(depth, n // tn),
        in_specs=[pl.BlockSpec((r, D_MODEL), lambda i, j: (0, 0)),
                  pl.BlockSpec((1, D_MODEL, tn), lambda i, j: (i, 0, j)),
                  pl.BlockSpec((1, 1, tn), lambda i, j: (i, 0, j))],
        out_specs=pl.BlockSpec((1, r, tn), lambda i, j: (i, 0, j)),
        out_shape=jax.ShapeDtypeStruct((depth, r, n), F32),
        compiler_params=_params("arbitrary", "arbitrary"),
        name="adaln",
    )(cond, ada_w, ada_b.reshape(depth, 1, n))


def _ffn_kernel(h_ref, mod_ref, g_ref, wg_ref, wu_ref, wo_ref, *rest, k, final_norm):
    if final_norm:
        fg_ref, o_ref, a_ref = rest
    else:
        o_ref, a_ref = rest
    j = pl.program_id(1)

    @pl.when(j == 0)
    def _():
        g, shift, scale = g_ref[...], mod_ref[0, 3 * k:3 * k + 1, :], mod_ref[0, 3 * k + 1:3 * k + 2, :]
        for r0 in range(0, a_ref.shape[0], NORM_ROWS):
            rs = slice(r0, r0 + NORM_ROWS)
            a_ref[rs, :] = _rms_mod(h_ref[rs, :], g, shift, scale).astype(BF16)
        o_ref[...] = jnp.zeros_like(o_ref)

    a = a_ref[...]
    act = (_silu(_dot(a, wg_ref[...])) * _dot(a, wu_ref[...])).astype(BF16)
    o_ref[...] += _dot(act, wo_ref[...])

    @pl.when(j == pl.num_programs(1) - 1)
    def _():
        h = h_ref[...] + (0.5 * mod_ref[0, 3 * k + 2:3 * k + 3, :]) * o_ref[...]
        if final_norm:
            h = h * lax.rsqrt(jnp.mean(h * h, axis=-1, keepdims=True) + RMS_EPS) * fg_ref[...]
        o_ref[...] = h


def ffn(h, mod, g, w_in, w_out, k, rows_per_mod, final_g=None):
    rows = h.shape[0]
    tm = _tile(rows_per_mod, FFN_ROWS)
    tf = _tile(D_FF, 512)
    nf = D_FF // tf
    per = rows_per_mod // tm
    vec_spec = pl.BlockSpec((1, D_MODEL), lambda i, j: (0, 0))
    in_specs = [pl.BlockSpec((tm, D_MODEL), lambda i, j: (i, 0)),
                pl.BlockSpec((1, N_MOD, D_MODEL), lambda i, j: (i // per, 0, 0)),
                vec_spec,
                pl.BlockSpec((D_MODEL, tf), lambda i, j: (0, j)),
                pl.BlockSpec((D_MODEL, tf), lambda i, j: (0, nf + j)),
                pl.BlockSpec((tf, D_MODEL), lambda i, j: (j, 0))]
    args = [h, mod, g.reshape(1, D_MODEL), w_in, w_in, w_out]
    if final_g is not None:
        in_specs.append(vec_spec)
        args.append(final_g.reshape(1, D_MODEL))
    return pl.pallas_call(
        functools.partial(_ffn_kernel, k=k, final_norm=final_g is not None),
        grid=(rows // tm, nf),
        in_specs=in_specs,
        out_specs=pl.BlockSpec((tm, D_MODEL), lambda i, j: (i, 0)),
        out_shape=jax.ShapeDtypeStruct((rows, D_MODEL), F32),
        scratch_shapes=[pltpu.VMEM((tm, D_MODEL), BF16)],
        compiler_params=_params("arbitrary", "arbitrary"),
        name="ffn",
    )(*args)


def _rope(x, cos, sin):
    lane = lax.broadcasted_iota(jnp.int32, x.shape, 1)
    partner = jnp.where(lane % 2 == 0, pltpu.roll(x, LANES - 1, 1), pltpu.roll(x, 1, 1))
    return x * cos + partner * sin


def _proj_kernel(h_ref, mod_ref, g_ref, w_ref, *rest, k, rope_tiles):
    if rope_tiles:
        cos_ref, sin_ref, o_ref, a_ref = rest
    else:
        o_ref, a_ref = rest
    j = pl.program_id(1)

    @pl.when(j == 0)
    def _():
        g, shift, scale = g_ref[...], mod_ref[0, 3 * k:3 * k + 1, :], mod_ref[0, 3 * k + 1:3 * k + 2, :]
        for r0 in range(0, a_ref.shape[0], NORM_ROWS):
            rs = slice(r0, r0 + NORM_ROWS)
            a_ref[rs, :] = _rms_mod(h_ref[rs, :], g, shift, scale).astype(BF16)

    res = _dot(a_ref[...], w_ref[...])
    if not rope_tiles:
        o_ref[...] = res.astype(o_ref.dtype)
        return
    rotary = j < rope_tiles
    cos = jnp.where(rotary, cos_ref[...], 1.0)
    sin = jnp.where(rotary, sin_ref[...], 0.0)
    for c in range(res.shape[1] // HEAD_DIM):
        sl = slice(c * HEAD_DIM, (c + 1) * HEAD_DIM)
        o_ref[:, sl] = _rope(res[:, sl], cos, sin).astype(o_ref.dtype)


def mod_proj(h, mod, g, w, k, rows_per_mod, tn, out_dtype, rope=None):
    rows = h.shape[0]
    n = w.shape[1]
    tm = _tile(rows_per_mod, 1024)
    per = rows_per_mod // tm
    in_specs = [pl.BlockSpec((tm, D_MODEL), lambda i, j: (i, 0)),
                pl.BlockSpec((1, N_MOD, D_MODEL), lambda i, j: (i // per, 0, 0)),
                pl.BlockSpec((1, D_MODEL), lambda i, j: (0, 0)),
                pl.BlockSpec((D_MODEL, tn), lambda i, j: (0, j))]
    args = [h, mod, g.reshape(1, D_MODEL), w]
    rope_tiles = 0
    if rope is not None:
        cos, sin, n_cols = rope
        rope_tiles = n_cols // tn
        in_specs += [pl.BlockSpec((tm, HEAD_DIM), lambda i, j: (i % per, 0))] * 2
        args += [cos, sin]
    return pl.pallas_call(
        functools.partial(_proj_kernel, k=k, rope_tiles=rope_tiles),
        grid=(rows // tm, n // tn),
        in_specs=in_specs,
        out_specs=pl.BlockSpec((tm, tn), lambda i, j: (i, j)),
        out_shape=jax.ShapeDtypeStruct((rows, n), out_dtype),
        scratch_shapes=[pltpu.VMEM((tm, D_MODEL), BF16)],
        compiler_params=_params("arbitrary", "arbitrary"),
        name="mod_proj",
    )(*args)


def _out_proj_kernel(x_ref, w_ref, h_ref, mod_ref, o_ref):
    o_ref[...] = h_ref[...] + mod_ref[0, 5:6, :] * _dot(x_ref[...], w_ref[...])


def out_proj(x, w, h, mod, rows_per_mod):
    rows, kdim = x.shape
    tm = _tile(rows_per_mod, 512)
    tn = _tile(D_MODEL, 1024)
    per = rows_per_mod // tm
    return pl.pallas_call(
        _out_proj_kernel,
        grid=(D_MODEL // tn, rows // tm),
        in_specs=[pl.BlockSpec((tm, kdim), lambda j, i: (i, 0)),
                  pl.BlockSpec((kdim, tn), lambda j, i: (0, j)),
                  pl.BlockSpec((tm, tn), lambda j, i: (i, j)),
                  pl.BlockSpec((1, N_MOD, tn), lambda j, i: (i // per, 0, j))],
        out_specs=pl.BlockSpec((tm, tn), lambda j, i: (i, j)),
        out_shape=jax.ShapeDtypeStruct((rows, D_MODEL), F32),
        compiler_params=_params("arbitrary", "arbitrary"),
        name="out_proj",
    )(x, w, h, mod)


def _ssm_in_kernel(prev_ref, h_ref, next_ref, mod_ref, g_ref, w_ref, wdt_ref, cw_ref, cb_ref,
                   o_ref, dt_ref, a_ref, res_ref, *, z_tiles):
    t = pl.program_id(1)
    j = pl.program_id(2)
    tm = h_ref.shape[1]
    halo = CONV_HALO
    groups = (tm + 2 * halo) // SUBLANES
    sub = lax.broadcasted_iota(jnp.int32, (1, SUBLANES, 1), 1)

    def shift_rows(x, up):
        if up:
            rot = pltpu.roll(x, SUBLANES - 1, 1)
            return jnp.where(sub == SUBLANES - 1, jnp.concatenate([rot[1:], rot[:1]], axis=0), rot)
        rot = pltpu.roll(x, 1, 1)
        return jnp.where(sub == 0, jnp.concatenate([rot[-1:], rot[:-1]], axis=0), rot)

    @pl.when(j == 0)
    def _():
        g, shift, scale = g_ref[...], mod_ref[0, 3:4, :], mod_ref[0, 4:5, :]
        a_ref[0:halo, :] = jnp.where(t > 0, _rms_mod(prev_ref[0], g, shift, scale), 0.0).astype(BF16)
        for r0 in range(0, tm, NORM_ROWS):
            a_ref[halo + r0:halo + r0 + NORM_ROWS, :] = _rms_mod(h_ref[0, r0:r0 + NORM_ROWS, :], g, shift,
                                                                   scale).astype(BF16)
        a_ref[halo + tm:, :] = jnp.where(t < pl.num_programs(1) - 1,
                                         _rms_mod(next_ref[0], g, shift, scale), 0.0).astype(BF16)
        dt_ref[0] = _dot(a_ref[halo:halo + tm, :], wdt_ref[...])

    @pl.when(j < z_tiles)
    def _():
        o_ref[0] = _dot(a_ref[halo:halo + tm, :], w_ref[...])

    @pl.when(j >= z_tiles)
    def _():
        res_ref[...] = _dot(a_ref[...], w_ref[...])
        for c in range(o_ref.shape[2] // LANES):
            cs = slice(c * LANES, (c + 1) * LANES)
            x = res_ref[:, cs].reshape(groups, SUBLANES, LANES)
            w = cw_ref[:, cs].reshape(SSM_CONV_W, 1, LANES)
            up = w[6:7] * x
            up = shift_rows(up, True) + w[5:6] * x
            up = shift_rows(up, True) + w[4:5] * x
            down = w[0:1] * x
            down = shift_rows(down, False) + w[1:2] * x
            down = shift_rows(down, False) + w[2:3] * x
            out = w[3:4] * x + shift_rows(up, True) + shift_rows(down, False) + cb_ref[:, cs].reshape(1, 1, LANES)
            first = halo // SUBLANES
            o_ref[0, :, cs] = _silu(out[first:first + tm // SUBLANES]).reshape(tm, LANES)


def ssm_in_proj(h, mod, g, w_zx, w_dt, conv_w, conv_b):
    bsz, n, _ = h.shape
    tm = _tile(n, SSM_IN_ROWS)
    tn = SSM_IN_COLS
    halo = CONV_HALO
    z_tiles = SSM_INNER // tn
    last_halo = n // halo - 1
    per_batch = mod.shape[0] > 1
    mod_idx = (lambda b, t, j: (b, 0, 0)) if per_batch else (lambda b, t, j: (0, 0, 0))
    conv_idx = lambda b, t, j: (0, jnp.maximum(j - z_tiles, 0))
    return pl.pallas_call(
        functools.partial(_ssm_in_kernel, z_tiles=z_tiles),
        grid=(bsz, n // tm, SSM_ZX // tn),
        in_specs=[pl.BlockSpec((1, halo, D_MODEL), lambda b, t, j: (b, jnp.maximum(t * (tm // halo) - 1, 0), 0)),
                  pl.BlockSpec((1, tm, D_MODEL), lambda b, t, j: (b, t, 0)),
                  pl.BlockSpec((1, halo, D_MODEL),
                               lambda b, t, j: (b, jnp.minimum((t + 1) * (tm // halo), last_halo), 0)),
                  pl.BlockSpec((1, N_MOD, D_MODEL), mod_idx),
                  pl.BlockSpec((1, D_MODEL), lambda b, t, j: (0, 0)),
                  pl.BlockSpec((D_MODEL, tn), lambda b, t, j: (0, j)),
                  pl.BlockSpec((D_MODEL, 2 * SSM_HEADS), lambda b, t, j: (0, 0)),
                  pl.BlockSpec((SSM_CONV_W, tn), conv_idx),
                  pl.BlockSpec((1, tn), conv_idx)],
        out_specs=[pl.BlockSpec((1, tm, tn), lambda b, t, j: (b, t, j)),
                   pl.BlockSpec((1, tm, 2 * SSM_HEADS), lambda b, t, j: (b, t, 0))],
        out_shape=[jax.ShapeDtypeStruct((bsz, n, SSM_ZX), F32),
                   jax.ShapeDtypeStruct((bsz, n, 2 * SSM_HEADS), F32)],
        scratch_shapes=[pltpu.VMEM((tm + 2 * halo, D_MODEL), BF16),
                        pltpu.VMEM((tm + 2 * halo, tn), F32)],
        compiler_params=_params("arbitrary", "arbitrary", "arbitrary"),
        name="ssm_in_proj",
    )(h, h, h, mod, g.reshape(1, D_MODEL), w_zx, w_dt, conv_w, conv_b.reshape(1, SSM_CONV_DIM))


def _ssd_kernel(x_ref, b_ref, c_ref, dtr_ref, dtb_ref, alog_ref, s0_ref, *rest, reverse):
    if reverse:
        yf_ref, z_ref, dskip_ref, gn_ref, y_ref, sfin_ref, st_ref = rest
    else:
        y_ref, sfin_ref, st_ref = rest
    c = pl.program_id(1)
    q = SSM_CHUNK

    @pl.when(c == 0)
    def _():
        st_ref[...] = s0_ref[0]

    pre = dtr_ref[0] + dtb_ref[...]
    dt = jnp.maximum(pre, 0.0) + jnp.log(1.0 + jnp.exp(-jnp.abs(pre)))
    da = dt * (-jnp.exp(alog_ref[...]))
    row = lax.broadcasted_iota(jnp.int32, (q, q), 0)
    col = lax.broadcasted_iota(jnp.int32, (q, q), 1)
    tri = (col >= row) if reverse else (col <= row)
    tri_bf = jnp.where(tri, 1.0, 0.0).astype(BF16)
    hi = da.astype(BF16)
    r1 = da - hi.astype(F32)
    mid = r1.astype(BF16)
    lo = (r1 - mid.astype(F32)).astype(BF16)
    cum = _dot(tri_bf, hi) + _dot(tri_bf, mid) + _dot(tri_bf, lo)
    cum_t = cum.T
    src_t = cum_t - jnp.log(dt.T)
    head0 = SSM_HEADS if reverse else 0
    last = 0 if reverse else q - 1
    is_a = col < SSM_HEAD_DIM
    lane_row = lax.broadcasted_iota(jnp.int32, (1, LANES), 1)

    for g in range(SSM_GROUPS):
        gs = slice(g * SSM_STATE, (g + 1) * SSM_STATE)
        bg = b_ref[0, :, gs]
        cg = c_ref[0, :, gs]
        bg_t = bg.T
        cg_bf = cg.astype(BF16)
        cb = _dot_nt(cg_bf, bg.astype(BF16))
        outs = []
        for pp in range(SSM_PAIRS_PER_GROUP):
            p = g * SSM_PAIRS_PER_GROUP + pp
            ps = slice(p * LANES, (p + 1) * LANES)
            xp = x_ref[0, :, ps]
            stp = st_ref[p]
            x_heads = [jnp.where(is_a, xp, 0.0).astype(BF16), jnp.where(is_a, 0.0, xp).astype(BF16)]
            ms, ws, decays, cum_cols = [], [], [], []
            for hh in range(2):
                hc = head0 + 2 * p + hh
                cum_col = jnp.broadcast_to(cum[:, hc:hc + 1], (q, q))
                src_row = src_t[hc:hc + 1, :]
                decay_dt = jnp.exp(jnp.where(tri, cum_col - src_row, -jnp.inf))
                ms.append((cb * decay_dt).astype(BF16))
                cum_last = cum_t[hc:hc + 1, last:last + 1]
                w_row = jnp.exp(cum_last - src_row)
                ws.append((bg_t * w_row).astype(BF16))
                decays.append(jnp.exp(cum_last))
                cum_cols.append(cum_col)
            y_state = jnp.exp(jnp.where(is_a, cum_cols[0], cum_cols[1])) * _dot(cg_bf, stp.astype(BF16))
            y_pair = _dot(jnp.concatenate(ms, axis=1), jnp.concatenate(x_heads, axis=0)) + y_state
            upd = _dot(jnp.concatenate(ws, axis=1), jnp.concatenate(x_heads, axis=0))
            chunk_decay = jnp.where(lane_row < SSM_HEAD_DIM, decays[0], decays[1])
            st_ref[p] = stp * chunk_decay + upd
            if reverse:
                y_tot = y_pair + yf_ref[0, :, ps] + dskip_ref[:, ps] * xp
                outs.append(y_tot * _silu(z_ref[0, :, ps]))
            else:
                y_ref[0, :, ps] = y_pair
        if reverse:
            ss = outs[0] * outs[0]
            for u in outs[1:]:
                ss = ss + u * u
            inv = lax.rsqrt(jnp.sum(ss, axis=-1, keepdims=True) / (SSM_PAIRS_PER_GROUP * LANES) + RMS_EPS)
            for pp, u in enumerate(outs):
                ps = slice((g * SSM_PAIRS_PER_GROUP + pp) * LANES, (g * SSM_PAIRS_PER_GROUP + pp + 1) * LANES)
                y_ref[0, :, ps] = (u * inv * gn_ref[:, ps]).astype(y_ref.dtype)

    @pl.when(c == pl.num_programs(1) - 1)
    def _():
        sfin_ref[0] = st_ref[...]


def ssd_scan(zx, dtr, dt_bias, a_log, state0, reverse, yf=None, d_skip=None, gn_g=None):
    bsz, n, _ = zx.shape
    q = SSM_CHUNK
    nc = n // q
    cidx = (lambda c: nc - 1 - c) if reverse else (lambda c: c)
    b_col = 2 * SSM_INNER // SSM_GN
    state_spec = pl.BlockSpec((1, SSM_PAIRS, SSM_STATE, LANES), lambda b, c: (b, 0, 0, 0))
    in_specs = [pl.BlockSpec((1, q, SSM_INNER), lambda b, c: (b, cidx(c), 1)),
                pl.BlockSpec((1, q, SSM_GN), lambda b, c: (b, cidx(c), b_col)),
                pl.BlockSpec((1, q, SSM_GN), lambda b, c: (b, cidx(c), b_col + 1)),
                pl.BlockSpec((1, q, 2 * SSM_HEADS), lambda b, c: (b, cidx(c), 0)),
                pl.BlockSpec((1, 2 * SSM_HEADS), lambda b, c: (0, 0)),
                pl.BlockSpec((1, 2 * SSM_HEADS), lambda b, c: (0, 0)),
                state_spec]
    args = [zx, zx, zx, dtr, dt_bias.reshape(1, 2 * SSM_HEADS), a_log.reshape(1, 2 * SSM_HEADS), state0]
    if reverse:
        in_specs += [pl.BlockSpec((1, q, SSM_INNER), lambda b, c: (b, cidx(c), 0)),
                     pl.BlockSpec((1, q, SSM_INNER), lambda b, c: (b, cidx(c), 0)),
                     pl.BlockSpec((1, SSM_INNER), lambda b, c: (0, 0)),
                     pl.BlockSpec((1, SSM_INNER), lambda b, c: (0, 0))]
        args += [yf, zx, jnp.repeat(d_skip, SSM_HEAD_DIM).reshape(1, SSM_INNER), gn_g.reshape(1, SSM_INNER)]
    return pl.pallas_call(
        functools.partial(_ssd_kernel, reverse=reverse),
        grid=(bsz, nc),
        in_specs=in_specs,
        out_specs=[pl.BlockSpec((1, q, SSM_INNER), lambda b, c: (b, cidx(c), 0)), state_spec],
        out_shape=[jax.ShapeDtypeStruct((bsz, n, SSM_INNER), BF16 if reverse else F32),
                   jax.ShapeDtypeStruct((bsz, SSM_PAIRS, SSM_STATE, LANES), F32)],
        scratch_shapes=[pltpu.VMEM((SSM_PAIRS, SSM_STATE, LANES), F32)],
        compiler_params=_params("arbitrary", "arbitrary"),
        name="ssd_bwd" if reverse else "ssd_fwd",
    )(*args)


def _pool_kernel(prev_ref, h_ref, next_ref, mod_ref, g_ref, w_ref, sc_ref, o_ref, buf_ref, *, n_seq):
    t = pl.program_id(1)
    tr = h_ref.shape[1]
    halo = POOL_HALO
    g = g_ref[...]
    shift, scale = mod_ref[0, 3:4, :], mod_ref[0, 4:5, :]
    h = h_ref[0]
    buf_ref[0:halo, :] = jnp.where(t > 0, _rms_mod(prev_ref[0], g, shift, scale), 0.0)
    buf_ref[halo:halo + tr, :] = _rms_mod(h, g, shift, scale)
    buf_ref[halo + tr:, :] = jnp.where(t < pl.num_programs(1) - 1, _rms_mod(next_ref[0], g, shift, scale), 0.0)
    pos = t * tr + lax.broadcasted_iota(jnp.int32, (tr, 1), 0)
    for grp, win in enumerate(POOL_WINDOWS):
        cs = slice(grp * POOL_GROUP_DIM, (grp + 1) * POOL_GROUP_DIM)
        a = buf_ref[:, cs]
        tot = a + pltpu.roll(a, 1, 0)
        step = 1
        while 2 * step < win:
            tot = pltpu.roll(tot, step, 0) + pltpu.roll(tot, tr + 2 * halo - step, 0)
            step *= 2
        tot = tot[halo:halo + tr]
        cnt =(jnp.minimum(pos - win // 2 + win, n_seq) - jnp.maximum(pos - win // 2, 0)).astype(F32)
        pooled = tot / cnt - buf_ref[halo:halo + tr, cs]
        y = _dot(pooled.astype(BF16), w_ref[grp]) * sc_ref[:, cs]
        o_ref[0, :, cs] = h[:, cs] + mod_ref[0, 5:6, cs] * y


def pool_mixer(h, mod, g, w_grp, scale):
    bsz, n, _ = h.shape
    tr = _tile(n, 256)
    halo = POOL_HALO
    last_halo = n // halo - 1
    per_batch = mod.shape[0] > 1
    mod_idx = (lambda b, t: (b, 0, 0)) if per_batch else (lambda b, t: (0, 0, 0))
    return pl.pallas_call(
        functools.partial(_pool_kernel, n_seq=n),
        grid=(bsz, n // tr),
        in_specs=[pl.BlockSpec((1, halo, D_MODEL), lambda b, t: (b, jnp.maximum(t * (tr // halo) - 1, 0), 0)),
                  pl.BlockSpec((1, tr, D_MODEL), lambda b, t: (b, t, 0)),
                  pl.BlockSpec((1, halo, D_MODEL),
                               lambda b, t: (b, jnp.minimum((t + 1) * (tr // halo), last_halo), 0)),
                  pl.BlockSpec((1, N_MOD, D_MODEL), mod_idx),
                  pl.BlockSpec((1, D_MODEL), lambda b, t: (0, 0)),
                  pl.BlockSpec((POOL_GROUPS, POOL_GROUP_DIM, POOL_GROUP_DIM), lambda b, t: (0, 0, 0)),
                  pl.BlockSpec((1, D_MODEL), lambda b, t: (0, 0))],
        out_specs=pl.BlockSpec((1, tr, D_MODEL), lambda b, t: (b, t, 0)),
        out_shape=jax.ShapeDtypeStruct((bsz, n, D_MODEL), F32),
        scratch_shapes=[pltpu.VMEM((tr + 2 * halo, D_MODEL), F32)],
        compiler_params=_params("arbitrary", "arbitrary"),
        name="pool_mixer",
    )(h, h, h, mod, g.reshape(1, D_MODEL), w_grp.astype(BF16), scale.reshape(1, D_MODEL))


def _attn_kernel(sink_ref, q_ref, kc_ref, vc_ref, *rest, band):
    if band:
        kp_ref, k0_ref, kn_ref, vp_ref, v0_ref, vn_ref, o_ref = rest
    else:
        (o_ref,) = rest
    blk = pl.program_id(1)
    nb = pl.num_programs(1)
    qb = ATT_BLOCK
    if band:
        shape = (Q_PER_KV * qb, 3 * qb)
        rows = lax.broadcasted_iota(jnp.int32, shape, 0) % qb
        cols = lax.broadcasted_iota(jnp.int32, shape, 1)
        diff = cols - rows
        ok = ((diff >= 0) & (diff <= 2 * WINDOW)
              & ((cols >= qb) | (blk > 0)) & ((cols < 2 * qb) | (blk < nb - 1)))
    for kvh in range(N_KV_HEADS):
        hs = slice(kvh * HEAD_DIM, (kvh + 1) * HEAD_DIM)
        q4 = jnp.concatenate([q_ref[0, :, (kvh * Q_PER_KV + g) * HEAD_DIM:(kvh * Q_PER_KV + g + 1) * HEAD_DIM]
                              for g in range(Q_PER_KV)], axis=0)
        sink = jnp.concatenate(
            [jnp.full((qb, 1), sink_ref[kvh * Q_PER_KV + g], F32) for g in range(Q_PER_KV)], axis=0)
        s_ctx = _dot_nt(q4, kc_ref[0, :, hs]) * ATT_SCALE
        m = jnp.maximum(jnp.max(s_ctx, axis=-1, keepdims=True), sink)
        if band:
            k_band = jnp.concatenate([kp_ref[0, :, hs], k0_ref[0, :, hs], kn_ref[0, :, hs]], axis=0)
            v_band = jnp.concatenate([vp_ref[0, :, hs], v0_ref[0, :, hs], vn_ref[0, :, hs]], axis=0)
            s_band = jnp.where(ok, _dot_nt(q4, k_band) * ATT_SCALE, -jnp.inf)
            m = jnp.maximum(m, jnp.max(s_band, axis=-1, keepdims=True))
        ones = jnp.ones((kc_ref.shape[1], HEAD_DIM), BF16)
        acc = _dot(jnp.exp(s_ctx - m).astype(BF16), jnp.concatenate([vc_ref[0, :, hs], ones], axis=1))
        if band:
            ones = jnp.ones((3 * qb, HEAD_DIM), BF16)
            acc = acc + _dot(jnp.exp(s_band - m).astype(BF16), jnp.concatenate([v_band, ones], axis=1))
        o4 = acc[:, :HEAD_DIM] / (acc[:, HEAD_DIM:] + jnp.exp(sink - m))
        for g in range(Q_PER_KV):
            gs = slice((kvh * Q_PER_KV + g) * HEAD_DIM, (kvh * Q_PER_KV + g + 1) * HEAD_DIM)
            o_ref[0, :, gs] = o4[g * qb:(g + 1) * qb, :].astype(o_ref.dtype)


def attention(qkv, qkv_ctx, sink, band):
    bsz, n, _ = qkv.shape
    n_ctx = qkv_ctx.shape[1]
    qb = ATT_BLOCK
    nb = n // qb
    k_col = Q_DIM // KV_DIM
    v_col = k_col + 1
    in_specs = [pl.BlockSpec(memory_space=pltpu.SMEM),
                pl.BlockSpec((1, qb, Q_DIM), lambda b, i: (b, i, 0)),
                pl.BlockSpec((1, n_ctx, KV_DIM), lambda b, i: (b, 0, k_col)),
                pl.BlockSpec((1, n_ctx, KV_DIM), lambda b, i: (b, 0, v_col))]
    args = [sink, qkv, qkv_ctx, qkv_ctx]
    if band:
        for col in (k_col, v_col):
            in_specs += [
                pl.BlockSpec((1, qb, KV_DIM), lambda b, i, col=col: (b, jnp.maximum(i - 1, 0), col)),
                pl.BlockSpec((1, qb, KV_DIM), lambda b, i, col=col: (b, i, col)),
                pl.BlockSpec((1, qb, KV_DIM), lambda b, i, col=col: (b, jnp.minimum(i + 1, nb - 1), col))]
            args += [qkv, qkv, qkv]
    return pl.pallas_call(
        functools.partial(_attn_kernel, band=band),
        grid=(bsz, nb),
        in_specs=in_specs,
        out_specs=pl.BlockSpec((1, qb, Q_DIM), lambda b, i: (b, i, 0)),
        out_shape=jax.ShapeDtypeStruct((bsz, n, Q_DIM), BF16),
        compiler_params=_params("arbitrary", "arbitrary"),
        name="attn_band" if band else "attn_ctx",
    )(*args)


def _rope_tables(n_tokens):
    rows = n_tokens // GRID_W
    row = jnp.repeat(jnp.arange(rows, dtype=jnp.int32), GRID_W)
    col = jnp.tile(jnp.arange(GRID_W, dtype=jnp.int32), rows)
    inv_freq = ROPE_BASE ** (-jnp.arange(0, ROPE_AXIS_DIM, 2, dtype=F32) / ROPE_AXIS_DIM)
    ang = jnp.concatenate([row.astype(F32)[:, None] * inv_freq, col.astype(F32)[:, None] * inv_freq], axis=-1)
    cos = jnp.repeat(jnp.cos(ang), 2, axis=-1)
    sin = jnp.stack([-jnp.sin(ang), jnp.sin(ang)], axis=-1).reshape(n_tokens, HEAD_DIM)
    return cos, sin


def kernel(x, c, ctx, c_ctx, ada_w, ada_b, norm_g, ffn_w_in, ffn_w_out, ssm_w_in, ssm_conv_w, ssm_conv_b,
           ssm_dt_bias, ssm_a_log, ssm_d, ssm_norm_g, ssm_w_out, pool_w, pool_scale, attn_w_qkv, attn_sink,
           attn_w_o, final_g):
    bsz, n_lat, _ = x.shape
    n_ctx = ctx.shape[1]
    depth = ada_w.shape[0]
    n_mixers = 3
    cond_rows = 2 * SUBLANES
    cond = jnp.zeros((cond_rows, D_MODEL), F32).at[:bsz].set(c).at[bsz].set(c_ctx)
    mods = adaln_all(cond, ada_w, ada_b)
    cos, sin = _rope_tables(n_lat)

    h = x.reshape(bsz * n_lat, D_MODEL)
    hc = ctx.reshape(bsz * n_ctx, D_MODEL)
    for i in range(depth):
        kind, j = i % n_mixers, i // n_mixers
        last = i == depth - 1
        ctx_live = (not last) or kind != 1
        ml = mods[i, :bsz].reshape(bsz, N_MOD, D_MODEL)
        mc = mods[i, bsz:bsz + 1].reshape(1, N_MOD, D_MODEL)
        w_in = [ffn_w_in[i, s].astype(BF16) for s in range(2)]
        w_out = [ffn_w_out[i, s].astype(BF16) for s in range(2)]

        h = ffn(h, ml, norm_g[i, 0], w_in[0], w_out[0], 0, n_lat)
        if ctx_live:
            hc = ffn(hc, mc, norm_g[i, 0], w_in[0], w_out[0], 0, bsz * n_ctx)

        if kind == 0:
            w_zx = ssm_w_in[j][:, :SSM_ZX].astype(BF16)
            w_dt = ssm_w_in[j][:, SSM_ZX:].astype(BF16)
            w_o = ssm_w_out[j].astype(BF16)
            proj = (norm_g[i, 1], w_zx, w_dt, ssm_conv_w[j], ssm_conv_b[j])
            zx_l, dt_l = ssm_in_proj(h.reshape(bsz, n_lat, D_MODEL), ml, *proj)
            zx_c, dt_c = ssm_in_proj(hc.reshape(bsz, n_ctx, D_MODEL), mc, *proj)
            zero = jnp.zeros((bsz, SSM_PAIRS, SSM_STATE, LANES), F32)
            fin = dict(d_skip=ssm_d[j], gn_g=ssm_norm_g[j])
            yf_c, sf = ssd_scan(zx_c, dt_c, ssm_dt_bias[j], ssm_a_log[j], zero, False)
            yn_c, sb = ssd_scan(zx_c, dt_c, ssm_dt_bias[j], ssm_a_log[j], zero, True, yf=yf_c, **fin)
            yf_l, _ = ssd_scan(zx_l, dt_l, ssm_dt_bias[j], ssm_a_log[j], sf, False)
            yn_l, _ = ssd_scan(zx_l, dt_l, ssm_dt_bias[j], ssm_a_log[j], sb, True, yf=yf_l, **fin)
            h = out_proj(yn_l.reshape(bsz * n_lat, SSM_INNER), w_o, h, ml, n_lat)
            if not last:
                hc = out_proj(yn_c.reshape(bsz * n_ctx, SSM_INNER), w_o, hc, mc, bsz * n_ctx)
        elif kind == 1:
            h = pool_mixer(h.reshape(bsz, n_lat, D_MODEL), ml, norm_g[i, 1], pool_w[j],
                           pool_scale[j]).reshape(bsz * n_lat, D_MODEL)
            if not last:
                hc = pool_mixer(hc.reshape(bsz, n_ctx, D_MODEL), mc, norm_g[i, 1], pool_w[j],
                                pool_scale[j]).reshape(bsz * n_ctx, D_MODEL)
        else:
            w_qkv = attn_w_qkv[j].astype(BF16)
            w_o = attn_w_o[j].astype(BF16)
            tn = 512
            qkv_l = mod_proj(h, ml, norm_g[i, 1], w_qkv, 1, n_lat, tn, BF16,
                             rope=(cos, sin, Q_DIM + KV_DIM)).reshape(bsz, n_lat, QKV_DIM)
            qkv_c = mod_proj(hc, mc, norm_g[i, 1], w_qkv, 1, bsz * n_ctx, tn, BF16).reshape(bsz, n_ctx, QKV_DIM)
            o_l = attention(qkv_l, qkv_c, attn_sink[j], True)
            h = out_proj(o_l.reshape(bsz * n_lat, Q_DIM), w_o, h, ml, n_lat)
            if not last:
                o_c = attention(qkv_c, qkv_c, attn_sink[j], False)
                hc = out_proj(o_c.reshape(bsz * n_ctx, Q_DIM), w_o, hc, mc, bsz * n_ctx)

        h = ffn(h, ml, norm_g[i, 2], w_in[1], w_out[1], 2, n_lat, final_g=final_g if last else None)
        if not last:
            hc = ffn(hc, mc, norm_g[i, 2], w_in[1], w_out[1], 2, bsz * n_ctx)
    return h.reshape(bsz, n_lat, D_MODEL)
```

```python
import functools

import jax
import jax.numpy as jnp
from jax import lax
from jax.experimental import pallas as pl
from jax.experimental.pallas import tpu as pltpu

D_MODEL = 2048
N_MOD = 9
RMS_EPS = 1e-6
D_FF = 5632
GRID_W = 64

SSM_INNER = 2 * D_MODEL
SSM_HEAD_DIM = 64
SSM_HEADS = SSM_INNER // SSM_HEAD_DIM
SSM_GROUPS = 8
SSM_STATE = 128
SSM_CONV_W = 7
SSM_CHUNK = 128
SSM_GN = SSM_GROUPS * SSM_STATE
SSM_CONV_DIM = SSM_INNER + 2 * SSM_GN
SSM_PROJ = SSM_INNER + SSM_CONV_DIM + 2 * SSM_HEADS
SSM_PAIRS = SSM_HEADS // 2
SSM_PAIRS_PER_GROUP = SSM_PAIRS // SSM_GROUPS
SSM_ZX = SSM_INNER + SSM_CONV_DIM
SSM_IN_ROWS = 1024
SSM_IN_COLS = 1024
CONV_HALO = 16
FFN_ROWS = 512
NORM_ROWS = 32

POOL_WINDOWS = (2, 4, 8, 16)
POOL_GROUPS = 4
POOL_GROUP_DIM = D_MODEL // POOL_GROUPS
POOL_HALO = 8

HEAD_DIM = 128
N_HEADS = D_MODEL // HEAD_DIM
N_KV_HEADS = 4
Q_PER_KV = N_HEADS // N_KV_HEADS
Q_DIM = N_HEADS * HEAD_DIM
KV_DIM = N_KV_HEADS * HEAD_DIM
QKV_DIM = Q_DIM + 2 * KV_DIM
WINDOW = 128
ATT_BLOCK = 128
ATT_SCALE = HEAD_DIM ** -0.5
ROPE_BASE = 10000.0
ROPE_AXIS_DIM = HEAD_DIM // 2

LANES = 128
SUBLANES = 8
VMEM_LIMIT_BYTES = 56 * 1024 * 1024

F32 = jnp.float32
BF16 = jnp.bfloat16


def _params(*semantics):
    return pltpu.CompilerParams(dimension_semantics=semantics, vmem_limit_bytes=VMEM_LIMIT_BYTES)


def _tile(n, preferred):
    t = min(n, preferred)
    while n % t:
        t //= 2
    return t


def _dot(a, b):
    return jnp.dot(a, b, preferred_element_type=F32)


def _dot_nt(a, b):
    return lax.dot_general(a, b, (((1,), (1,)), ((), ())), preferred_element_type=F32)


def _sigmoid(x):
    return 1.0 / (1.0 + jnp.exp(-x))


def _silu(x):
    return x * _sigmoid(x)


def _rms_mod(h, g, shift, scale):
    ms = jnp.mean(h * h, axis=-1, keepdims=True)
    return (h * lax.rsqrt(ms + RMS_EPS)) * (g * (1.0 + scale)) + shift


def _adaln_kernel(c_ref, w_ref, b_ref, o_ref):
    s = _silu(c_ref[...]).astype(BF16)
    o_ref[0] = _dot(s, w_ref[0].astype(BF16)) + b_ref[0]


def adaln_all(cond, ada_w, ada_b):
    depth, _, n = ada_w.shape
    r = cond.shape[0]
    tn = _tile(n, 1024)
    return pl.pallas_call(
        _adaln_kernel,
        grid=(depth, n // tn),
        in_specs=[pl.BlockSpec((r, D_MODEL), lambda i, j: (0, 0)),
                  pl.BlockSpec((1, D_MODEL, tn), lambda i, j: (i, 0, j)),
                  pl.BlockSpec((1, 1, tn), lambda i, j: (i, 0, j))],
        out_specs=pl.BlockSpec((1, r, tn), lambda i, j: (i, 0, j)),
        out_shape=jax.ShapeDtypeStruct((depth, r, n), F32),
        compiler_params=_params("arbitrary", "arbitrary"),
        name="adaln",
    )(cond, ada_w, ada_b.reshape(depth, 1, n))


def _ffn_kernel(h_ref, mod_ref, g_ref, wg_ref, wu_ref, wo_ref, *rest, k, final_norm):
    if final_norm:
        fg_ref, o_ref, a_ref = rest
    else:
        o_ref, a_ref = rest
    j = pl.program_id(1)

    @pl.when(j == 0)
    def _():
        g, shift, scale = g_ref[...], mod_ref[0, 3 * k:3 * k + 1, :], mod_ref[0, 3 * k + 1:3 * k + 2, :]
        for r0 in range(0, a_ref.shape[0], NORM_ROWS):
            rs = slice(r0, r0 + NORM_ROWS)
            a_ref[rs, :] = _rms_mod(h_ref[rs, :], g, shift, scale).astype(BF16)
        o_ref[...] = jnp.zeros_like(o_ref)

    a = a_ref[...]
    act = (_silu(_dot(a, wg_ref[...])) * _dot(a, wu_ref[...])).astype(BF16)
    o_ref[...] += _dot(act, wo_ref[...])

    @pl.when(j == pl.num_programs(1) - 1)
    def _():
        h = h_ref[...] + (0.5 * mod_ref[0, 3 * k + 2:3 * k + 3, :]) * o_ref[...]
        if final_norm:
            h = h * lax.rsqrt(jnp.mean(h * h, axis=-1, keepdims=True) + RMS_EPS) * fg_ref[...]
        o_ref[...] = h


def ffn(h, mod, g, w_in, w_out, layer, slot, k, rows_per_mod, final_g=None):
    rows = h.shape[0]
    tm = _tile(rows_per_mod, FFN_ROWS)
    tf = _tile(D_FF, 512)
    nf = D_FF // tf
    per = rows_per_mod // tm
    vec_spec = pl.BlockSpec((1, D_MODEL), lambda i, j: (0, 0))
    in_specs = [pl.BlockSpec((tm, D_MODEL), lambda i, j: (i, 0)),
                pl.BlockSpec((1, N_MOD, D_MODEL), lambda i, j: (i // per, 0, 0)),
                vec_spec,
                pl.BlockSpec((None, None, D_MODEL, tf), lambda i, j: (layer, slot, 0, j)),
                pl.BlockSpec((None, None, D_MODEL, tf), lambda i, j: (layer, slot, 0, nf + j)),
                pl.BlockSpec((None, None, tf, D_MODEL), lambda i, j: (layer, slot, j, 0))]
    args = [h, mod, g.reshape(1, D_MODEL), w_in, w_in, w_out]
    if final_g is not None:
        in_specs.append(vec_spec)
        args.append(final_g.reshape(1, D_MODEL))
    return pl.pallas_call(
        functools.partial(_ffn_kernel, k=k, final_norm=final_g is not None),
        grid=(rows // tm, nf),
        in_specs=in_specs,
        out_specs=pl.BlockSpec((tm, D_MODEL), lambda i, j: (i, 0)),
        out_shape=jax.ShapeDtypeStruct((rows, D_MODEL), F32),
        scratch_shapes=[pltpu.VMEM((tm, D_MODEL), BF16)],
        compiler_params=_params("arbitrary", "arbitrary"),
        name="ffn",
    )(*args)


def _rope(x, cos, sin):
    lane = lax.broadcasted_iota(jnp.int32, x.shape, 1)
    partner = jnp.where(lane % 2 == 0, pltpu.roll(x, LANES - 1, 1), pltpu.roll(x, 1, 1))
    return x * cos + partner * sin


def _proj_kernel(h_ref, mod_ref, g_ref, w_ref, *rest, k, rope_tiles):
    if rope_tiles:
        cos_ref, sin_ref, o_ref, a_ref = rest
    else:
        o_ref, a_ref = rest
    j = pl.program_id(1)

    @pl.when(j == 0)
    def _():
        g, shift, scale = g_ref[...], mod_ref[0, 3 * k:3 * k + 1, :], mod_ref[0, 3 * k + 1:3 * k + 2, :]
        for r0 in range(0, a_ref.shape[0], NORM_ROWS):
            rs = slice(r0, r0 + NORM_ROWS)
            a_ref[rs, :] = _rms_mod(h_ref[rs, :], g, shift, scale).astype(BF16)

    res = _dot(a_ref[...], w_ref[...])
    if not rope_tiles:
        o_ref[...] = res.astype(o_ref.dtype)
        return
    rotary = j < rope_tiles
    cos = jnp.where(rotary, cos_ref[...], 1.0)
    sin = jnp.where(rotary, sin_ref[...], 0.0)
    for c in range(res.shape[1] // HEAD_DIM):
        sl = slice(c * HEAD_DIM, (c + 1) * HEAD_DIM)
        o_ref[:, sl] = _rope(res[:, sl], cos, sin).astype(o_ref.dtype)


def mod_proj(h, mod, g, w, k, rows_per_mod, tn, out_dtype, rope=None):
    rows = h.shape[0]
    n = w.shape[1]
    tm = _tile(rows_per_mod, 1024)
    per = rows_per_mod // tm
    in_specs = [pl.BlockSpec((tm, D_MODEL), lambda i, j: (i, 0)),
                pl.BlockSpec((1, N_MOD, D_MODEL), lambda i, j: (i // per, 0, 0)),
                pl.BlockSpec((1, D_MODEL), lambda i, j: (0, 0)),
                pl.BlockSpec((D_MODEL, tn), lambda i, j: (0, j))]
    args = [h, mod, g.reshape(1, D_MODEL), w]
    rope_tiles = 0
    if rope is not None:
        cos, sin, n_cols = rope
        rope_tiles = n_cols // tn
        in_specs += [pl.BlockSpec((tm, HEAD_DIM), lambda i, j: (i % per, 0))] * 2
        args += [cos, sin]
    return pl.pallas_call(
        functools.partial(_proj_kernel, k=k, rope_tiles=rope_tiles),
        grid=(rows // tm, n // tn),
        in_specs=in_specs,
        out_specs=pl.BlockSpec((tm, tn), lambda i, j: (i, j)),
        out_shape=jax.ShapeDtypeStruct((rows, n), out_dtype),
        scratch_shapes=[pltpu.VMEM((tm, D_MODEL), BF16)],
        compiler_params=_params("arbitrary", "arbitrary"),
        name="mod_proj",
    )(*args)


def _out_proj_kernel(x_ref, w_ref, h_ref, mod_ref, o_ref):
    o_ref[...] = h_ref[...] + mod_ref[0, 5:6, :] * _dot(x_ref[...], w_ref[...])


def out_proj(x, w, h, mod, rows_per_mod):
    rows, kdim = x.shape
    tm = _tile(rows_per_mod, 512)
    tn = _tile(D_MODEL, 1024)
    per = rows_per_mod // tm
    return pl.pallas_call(
        _out_proj_kernel,
        grid=(D_MODEL // tn, rows // tm),
        in_specs=[pl.BlockSpec((tm, kdim), lambda j, i: (i, 0)),
                  pl.BlockSpec((kdim, tn), lambda j, i: (0, j)),
                  pl.BlockSpec((tm, tn), lambda j, i: (i, j)),
                  pl.BlockSpec((1, N_MOD, tn), lambda j, i: (i // per, 0, j))],
        out_specs=pl.BlockSpec((tm, tn), lambda j, i: (i, j)),
        out_shape=jax.ShapeDtypeStruct((rows, D_MODEL), F32),
        compiler_params=_params("arbitrary", "arbitrary"),
        name="out_proj",
    )(x, w, h, mod)


def _ssm_in_kernel(prev_ref, h_ref, next_ref, mod_ref, g_ref, w_ref, wdt_ref, cw_ref, cb_ref,
                   o_ref, dt_ref, a_ref, res_ref, *, z_tiles):
    t = pl.program_id(1)
    j = pl.program_id(2)
    tm = h_ref.shape[1]
    halo = CONV_HALO
    groups = (tm + 2 * halo) // SUBLANES
    sub = lax.broadcasted_iota(jnp.int32, (1, SUBLANES, 1), 1)

    def shift_rows(x, up):
        if up:
            rot = pltpu.roll(x, SUBLANES - 1, 1)
            return jnp.where(sub == SUBLANES - 1, jnp.concatenate([rot[1:], rot[:1]], axis=0), rot)
        rot = pltpu.roll(x, 1, 1)
        return jnp.where(sub == 0, jnp.concatenate([rot[-1:], rot[:-1]], axis=0), rot)

    @pl.when(j == 0)
    def _():
        g, shift, scale = g_ref[...], mod_ref[0, 3:4, :], mod_ref[0, 4:5, :]
        a_ref[0:halo, :] = jnp.where(t > 0, _rms_mod(prev_ref[0], g, shift, scale), 0.0).astype(BF16)
        for r0 in range(0, tm, NORM_ROWS):
            a_ref[halo + r0:halo + r0 + NORM_ROWS, :] = _rms_mod(h_ref[0, r0:r0 + NORM_ROWS, :], g, shift,
                                                                   scale).astype(BF16)
        a_ref[halo + tm:, :] = jnp.where(t < pl.num_programs(1) - 1,
                                         _rms_mod(next_ref[0], g, shift, scale), 0.0).astype(BF16)
        dt_ref[0] = _dot(a_ref[halo:halo + tm, :], wdt_ref[...])

    @pl.when(j < z_tiles)
    def _():
        o_ref[0] = _dot(a_ref[halo:halo + tm, :], w_ref[...])

    @pl.when(j >= z_tiles)
    def _():
        res_ref[...] = _dot(a_ref[...], w_ref[...])
        for c in range(o_ref.shape[2] // LANES):
            cs = slice(c * LANES, (c + 1) * LANES)
            x = res_ref[:, cs].reshape(groups, SUBLANES, LANES)
            w = cw_ref[:, cs].reshape(SSM_CONV_W, 1, LANES)
            up = w[6:7] * x
            up = shift_rows(up, True) + w[5:6] * x
            up = shift_rows(up, True) + w[4:5] * x
            down = w[0:1] * x
            down = shift_rows(down, False) + w[1:2] * x
            down = shift_rows(down, False) + w[2:3] * x
            out = w[3:4] * x + shift_rows(up, True) + shift_rows(down, False) + cb_ref[:, cs].reshape(1, 1, LANES)
            first = halo // SUBLANES
            o_ref[0, :, cs] = _silu(out[first:first + tm // SUBLANES]).reshape(tm, LANES)


def ssm_in_proj(h, mod, g, w_zx, w_dt, conv_w, conv_b):
    bsz, n, _ = h.shape
    tm = _tile(n, SSM_IN_ROWS)
    tn = SSM_IN_COLS
    halo = CONV_HALO
    z_tiles = SSM_INNER // tn
    last_halo = n // halo - 1
    per_batch = mod.shape[0] > 1
    mod_idx = (lambda b, t, j: (b, 0, 0)) if per_batch else (lambda b, t, j: (0, 0, 0))
    conv_idx = lambda b, t, j: (0, jnp.maximum(j - z_tiles, 0))
    return pl.pallas_call(
        functools.partial(_ssm_in_kernel, z_tiles=z_tiles),
        grid=(bsz, n // tm, SSM_ZX // tn),
        in_specs=[pl.BlockSpec((1, halo, D_MODEL), lambda b, t, j: (b, jnp.maximum(t * (tm // halo) - 1, 0), 0)),
                  pl.BlockSpec((1, tm, D_MODEL), lambda b, t, j: (b, t, 0)),
                  pl.BlockSpec((1, halo, D_MODEL),
                               lambda b, t, j: (b, jnp.minimum((t + 1) * (tm // halo), last_halo), 0)),
                  pl.BlockSpec((1, N_MOD, D_MODEL), mod_idx),
                  pl.BlockSpec((1, D_MODEL), lambda b, t, j: (0, 0)),
                  pl.BlockSpec((D_MODEL, tn), lambda b, t, j: (0, j)),
                  pl.BlockSpec((D_MODEL, 2 * SSM_HEADS), lambda b, t, j: (0, 0)),
                  pl.BlockSpec((SSM_CONV_W, tn), conv_idx),
                  pl.BlockSpec((1, tn), conv_idx)],
        out_specs=[pl.BlockSpec((1, tm, tn), lambda b, t, j: (b, t, j)),
                   pl.BlockSpec((1, tm, 2 * SSM_HEADS), lambda b, t, j: (b, t, 0))],
        out_shape=[jax.ShapeDtypeStruct((bsz, n, SSM_ZX), F32),
                   jax.ShapeDtypeStruct((bsz, n, 2 * SSM_HEADS), F32)],
        scratch_shapes=[pltpu.VMEM((tm + 2 * halo, D_MODEL), BF16),
                        pltpu.VMEM((tm + 2 * halo, tn), F32)],
        compiler_params=_params("arbitrary", "arbitrary", "arbitrary"),
        name="ssm_in_proj",
    )(h, h, h, mod, g.reshape(1, D_MODEL), w_zx, w_dt, conv_w, conv_b.reshape(1, SSM_CONV_DIM))


def _ssd_kernel(x_ref, b_ref, c_ref, dtr_ref, dtb_ref, alog_ref, s0_ref, *rest, reverse):
    if reverse:
        yf_ref, z_ref, dskip_ref, gn_ref, y_ref, sfin_ref, st_ref = rest
    else:
        y_ref, sfin_ref, st_ref = rest
    c = pl.program_id(1)
    q = SSM_CHUNK

    @pl.when(c == 0)
    def _():
        st_ref[...] = s0_ref[0]

    pre = dtr_ref[0] + dtb_ref[...]
    dt = jnp.maximum(pre, 0.0) + jnp.log(1.0 + jnp.exp(-jnp.abs(pre)))
    da = dt * (-jnp.exp(alog_ref[...]))
    row = lax.broadcasted_iota(jnp.int32, (q, q), 0)
    col = lax.broadcasted_iota(jnp.int32, (q, q), 1)
    tri = (col >= row) if reverse else (col <= row)
    tri_bf = jnp.where(tri, 1.0, 0.0).astype(BF16)
    hi = da.astype(BF16)
    r1 = da - hi.astype(F32)
    mid = r1.astype(BF16)
    lo = (r1 - mid.astype(F32)).astype(BF16)
    cum = _dot(tri_bf, hi) + _dot(tri_bf, mid) + _dot(tri_bf, lo)
    cum_t = cum.T
    src_t = cum_t - jnp.log(dt.T)
    head0 = SSM_HEADS if reverse else 0
    last = 0 if reverse else q - 1
    is_a = col < SSM_HEAD_DIM
    lane_row = lax.broadcasted_iota(jnp.int32, (1, LANES), 1)

    for g in range(SSM_GROUPS):
        gs = slice(g * SSM_STATE, (g + 1) * SSM_STATE)
        bg = b_ref[0, :, gs]
        cg = c_ref[0, :, gs]
        bg_t = bg.T
        cg_bf = cg.astype(BF16)
        cb = _dot_nt(cg_bf, bg.astype(BF16))
        outs = []
        for pp in range(SSM_PAIRS_PER_GROUP):
            p = g * SSM_PAIRS_PER_GROUP + pp
            ps = slice(p * LANES, (p + 1) * LANES)
            xp = x_ref[0, :, ps]
            stp = st_ref[p]
            x_heads = [jnp.where(is_a, xp, 0.0).astype(BF16), jnp.where(is_a, 0.0, xp).astype(BF16)]
            ms, ws, decays, cum_cols = [], [], [], []
            for hh in range(2):
                hc = head0 + 2 * p + hh
                cum_col = jnp.broadcast_to(cum[:, hc:hc + 1], (q, q))
                src_row = src_t[hc:hc + 1, :]
                decay_dt = jnp.exp(jnp.where(tri, cum_col - src_row, -jnp.inf))
                ms.append((cb * decay_dt).astype(BF16))
                cum_last = cum_t[hc:hc + 1, last:last + 1]
                w_row = jnp.exp(cum_last - src_row)
                ws.append((bg_t * w_row).astype(BF16))
                decays.append(jnp.exp(cum_last))
                cum_cols.append(cum_col)
            y_state = jnp.exp(jnp.where(is_a, cum_cols[0], cum_cols[1])) * _dot(cg_bf, stp.astype(BF16))
            y_pair = _dot(jnp.concatenate(ms, axis=1), jnp.concatenate(x_heads, axis=0)) + y_state
            upd = _dot(jnp.concatenate(ws, axis=1), jnp.concatenate(x_heads, axis=0))
            chunk_decay = jnp.where(lane_row < SSM_HEAD_DIM, decays[0], decays[1])
            st_ref[p] = stp * chunk_decay + upd
            if reverse:
                y_tot = y_pair + yf_ref[0, :, ps] + dskip_ref[:, ps] * xp
                outs.append(y_tot * _silu(z_ref[0, :, ps]))
            else:
                y_ref[0, :, ps] = y_pair
        if reverse:
            ss = outs[0] * outs[0]
            for u in outs[1:]:
                ss = ss + u * u
            inv = lax.rsqrt(jnp.sum(ss, axis=-1, keepdims=True) / (SSM_PAIRS_PER_GROUP * LANES) + RMS_EPS)
            for pp, u in enumerate(outs):
                ps = slice((g * SSM_PAIRS_PER_GROUP + pp) * LANES, (g * SSM_PAIRS_PER_GROUP + pp + 1) * LANES)
                y_ref[0, :, ps] = (u * inv * gn_ref[:, ps]).astype(y_ref.dtype)

    @pl.when(c == pl.num_programs(1) - 1)
    def _():
        sfin_ref[0] = st_ref[...]


def ssd_scan(zx, dtr, dt_bias, a_log, state0, reverse, yf=None, d_skip=None, gn_g=None):
    bsz, n, _ = zx.shape
    q = SSM_CHUNK
    nc = n // q
    cidx = (lambda c: nc - 1 - c) if reverse else (lambda c: c)
    b_col = 2 * SSM_INNER // SSM_GN
    state_spec = pl.BlockSpec((1, SSM_PAIRS, SSM_STATE, LANES), lambda b, c: (b, 0, 0, 0))
    in_specs = [pl.BlockSpec((1, q, SSM_INNER), lambda b, c: (b, cidx(c), 1)),
                pl.BlockSpec((1, q, SSM_GN), lambda b, c: (b, cidx(c), b_col)),
                pl.BlockSpec((1, q, SSM_GN), lambda b, c: (b, cidx(c), b_col + 1)),
                pl.BlockSpec((1, q, 2 * SSM_HEADS), lambda b, c: (b, cidx(c), 0)),
                pl.BlockSpec((1, 2 * SSM_HEADS), lambda b, c: (0, 0)),
                pl.BlockSpec((1, 2 * SSM_HEADS), lambda b, c: (0, 0)),
                state_spec]
    args = [zx, zx, zx, dtr, dt_bias.reshape(1, 2 * SSM_HEADS), a_log.reshape(1, 2 * SSM_HEADS), state0]
    if reverse:
        in_specs += [pl.BlockSpec((1, q, SSM_INNER), lambda b, c: (b, cidx(c), 0)),
                     pl.BlockSpec((1, q, SSM_INNER), lambda b, c: (b, cidx(c), 0)),
                     pl.BlockSpec((1, SSM_INNER), lambda b, c: (0, 0)),
                     pl.BlockSpec((1, SSM_INNER), lambda b, c: (0, 0))]
        args += [yf, zx, jnp.repeat(d_skip, SSM_HEAD_DIM).reshape(1, SSM_INNER), gn_g.reshape(1, SSM_INNER)]
    return pl.pallas_call(
        functools.partial(_ssd_kernel, reverse=reverse),
        grid=(bsz, nc),
        in_specs=in_specs,
        out_specs=[pl.BlockSpec((1, q, SSM_INNER), lambda b, c: (b, cidx(c), 0)), state_spec],
        out_shape=[jax.ShapeDtypeStruct((bsz, n, SSM_INNER), BF16 if reverse else F32),
                   jax.ShapeDtypeStruct((bsz, SSM_PAIRS, SSM_STATE, LANES), F32)],
        scratch_shapes=[pltpu.VMEM((SSM_PAIRS, SSM_STATE, LANES), F32)],
        compiler_params=_params("arbitrary", "arbitrary"),
        name="ssd_bwd" if reverse else "ssd_fwd",
    )(*args)


def _pool_kernel(prev_ref, h_ref, next_ref, mod_ref, g_ref, w_ref, sc_ref, o_ref, buf_ref, *, n_seq):
    t = pl.program_id(1)
    tr = h_ref.shape[1]
    halo = POOL_HALO
    g = g_ref[...]
    shift, scale = mod_ref[0, 3:4, :], mod_ref[0, 4:5, :]
    h = h_ref[0]
    buf_ref[0:halo, :] = jnp.where(t > 0, _rms_mod(prev_ref[0], g, shift, scale), 0.0)
    buf_ref[halo:halo + tr, :] = _rms_mod(h, g, shift, scale)
    buf_ref[halo + tr:, :] = jnp.where(t < pl.num_programs(1) - 1, _rms_mod(next_ref[0], g, shift, scale), 0.0)
    pos = t * tr + lax.broadcasted_iota(jnp.int32, (tr, 1), 0)
    for grp, win in enumerate(POOL_WINDOWS):
        cs = slice(grp * POOL_GROUP_DIM, (grp + 1) * POOL_GROUP_DIM)
        a = buf_ref[:, cs]
        tot = a + pltpu.roll(a, 1, 0)
        step = 1
        while 2 * step < win:
            tot = pltpu.roll(tot, step, 0) + pltpu.roll(tot, tr + 2 * halo - step, 0)
            step *= 2
        tot = tot[halo:halo + tr]
        cnt =(jnp.minimum(pos - win // 2 + win, n_seq) - jnp.maximum(pos - win // 2, 0)).astype(F32)
        pooled = tot / cnt - buf_ref[halo:halo + tr, cs]
        y = _dot(pooled.astype(BF16), w_ref[grp]) * sc_ref[:, cs]
        o_ref[0, :, cs] = h[:, cs] + mod_ref[0, 5:6, cs] * y


def pool_mixer(h, mod, g, w_grp, scale):
    bsz, n, _ = h.shape
    tr = _tile(n, 256)
    halo = POOL_HALO
    last_halo = n // halo - 1
    per_batch = mod.shape[0] > 1
    mod_idx = (lambda b, t: (b, 0, 0)) if per_batch else (lambda b, t: (0, 0, 0))
    return pl.pallas_call(
        functools.partial(_pool_kernel, n_seq=n),
        grid=(bsz, n // tr),
        in_specs=[pl.BlockSpec((1, halo, D_MODEL), lambda b, t: (b, jnp.maximum(t * (tr // halo) - 1, 0), 0)),
                  pl.BlockSpec((1, tr, D_MODEL), lambda b, t: (b, t, 0)),
                  pl.BlockSpec((1, halo, D_MODEL),
                               lambda b, t: (b, jnp.minimum((t + 1) * (tr // halo), last_halo), 0)),
                  pl.BlockSpec((1, N_MOD, D_MODEL), mod_idx),
                  pl.BlockSpec((1, D_MODEL), lambda b, t: (0, 0)),
                  pl.BlockSpec((POOL_GROUPS, POOL_GROUP_DIM, POOL_GROUP_DIM), lambda b, t: (0, 0, 0)),
                  pl.BlockSpec((1, D_MODEL), lambda b, t: (0, 0))],
        out_specs=pl.BlockSpec((1, tr, D_MODEL), lambda b, t: (b, t, 0)),
        out_shape=jax.ShapeDtypeStruct((bsz, n, D_MODEL), F32),
        scratch_shapes=[pltpu.VMEM((tr + 2 * halo, D_MODEL), F32)],
        compiler_params=_params("arbitrary", "arbitrary"),
        name="pool_mixer",
    )(h, h, h, mod, g.reshape(1, D_MODEL), w_grp.astype(BF16), scale.reshape(1, D_MODEL))


def _attn_kernel(sink_ref, q_ref, kc_ref, vc_ref, *rest, band):
    if band:
        kp_ref, k0_ref, kn_ref, vp_ref, v0_ref, vn_ref, o_ref = rest
    else:
        (o_ref,) = rest
    blk = pl.program_id(1)
    nb = pl.num_programs(1)
    qb = ATT_BLOCK
    if band:
        shape = (Q_PER_KV * qb, 3 * qb)
        rows = lax.broadcasted_iota(jnp.int32, shape, 0) % qb
        cols = lax.broadcasted_iota(jnp.int32, shape, 1)
        diff = cols - rows
        ok = ((diff >= 0) & (diff <= 2 * WINDOW)
              & ((cols >= qb) | (blk > 0)) & ((cols < 2 * qb) | (blk < nb - 1)))
    for kvh in range(N_KV_HEADS):
        hs = slice(kvh * HEAD_DIM, (kvh + 1) * HEAD_DIM)
        q4 = jnp.concatenate([q_ref[0, :, (kvh * Q_PER_KV + g) * HEAD_DIM:(kvh * Q_PER_KV + g + 1) * HEAD_DIM]
                              for g in range(Q_PER_KV)], axis=0)
        sink = jnp.concatenate(
            [jnp.full((qb, 1), sink_ref[kvh * Q_PER_KV + g], F32) for g in range(Q_PER_KV)], axis=0)
        s_ctx = _dot_nt(q4, kc_ref[0, :, hs]) * ATT_SCALE
        m = jnp.maximum(jnp.max(s_ctx, axis=-1, keepdims=True), sink)
        if band:
            k_band = jnp.concatenate([kp_ref[0, :, hs], k0_ref[0, :, hs], kn_ref[0, :, hs]], axis=0)
            v_band = jnp.concatenate([vp_ref[0, :, hs], v0_ref[0, :, hs], vn_ref[0, :, hs]], axis=0)
            s_band = jnp.where(ok, _dot_nt(q4, k_band) * ATT_SCALE, -jnp.inf)
            m = jnp.maximum(m, jnp.max(s_band, axis=-1, keepdims=True))
        ones = jnp.ones((kc_ref.shape[1], HEAD_DIM), BF16)
        acc = _dot(jnp.exp(s_ctx - m).astype(BF16), jnp.concatenate([vc_ref[0, :, hs], ones], axis=1))
        if band:
            ones = jnp.ones((3 * qb, HEAD_DIM), BF16)
            acc = acc + _dot(jnp.exp(s_band - m).astype(BF16), jnp.concatenate([v_band, ones], axis=1))
        o4 = acc[:, :HEAD_DIM] / (acc[:, HEAD_DIM:] + jnp.exp(sink - m))
        for g in range(Q_PER_KV):
            gs = slice((kvh * Q_PER_KV + g) * HEAD_DIM, (kvh * Q_PER_KV + g + 1) * HEAD_DIM)
            o_ref[0, :, gs] = o4[g * qb:(g + 1) * qb, :].astype(o_ref.dtype)


def attention(qkv, qkv_ctx, sink, band):
    bsz, n, _ = qkv.shape
    n_ctx = qkv_ctx.shape[1]
    qb = ATT_BLOCK
    nb = n // qb
    k_col = Q_DIM // KV_DIM
    v_col = k_col + 1
    in_specs = [pl.BlockSpec(memory_space=pltpu.SMEM),
                pl.BlockSpec((1, qb, Q_DIM), lambda b, i: (b, i, 0)),
                pl.BlockSpec((1, n_ctx, KV_DIM), lambda b, i: (b, 0, k_col)),
                pl.BlockSpec((1, n_ctx, KV_DIM), lambda b, i: (b, 0, v_col))]
    args = [sink, qkv, qkv_ctx, qkv_ctx]
    if band:
        for col in (k_col, v_col):
            in_specs += [
                pl.BlockSpec((1, qb, KV_DIM), lambda b, i, col=col: (b, jnp.maximum(i - 1, 0), col)),
                pl.BlockSpec((1, qb, KV_DIM), lambda b, i, col=col: (b, i, col)),
                pl.BlockSpec((1, qb, KV_DIM), lambda b, i, col=col: (b, jnp.minimum(i + 1, nb - 1), col))]
            args += [qkv, qkv, qkv]
    return pl.pallas_call(
        functools.partial(_attn_kernel, band=band),
        grid=(bsz, nb),
        in_specs=in_specs,
        out_specs=pl.BlockSpec((1, qb, Q_DIM), lambda b, i: (b, i, 0)),
        out_shape=jax.ShapeDtypeStruct((bsz, n, Q_DIM), BF16),
        compiler_params=_params("arbitrary", "arbitrary"),
        name="attn_band" if band else "attn_ctx",
    )(*args)


def _rope_tables(n_tokens):
    rows = n_tokens // GRID_W
    row = jnp.repeat(jnp.arange(rows, dtype=jnp.int32), GRID_W)
    col = jnp.tile(jnp.arange(GRID_W, dtype=jnp.int32), rows)
    inv_freq = ROPE_BASE ** (-jnp.arange(0, ROPE_AXIS_DIM, 2, dtype=F32) / ROPE_AXIS_DIM)
    ang = jnp.concatenate([row.astype(F32)[:, None] * inv_freq, col.astype(F32)[:, None] * inv_freq], axis=-1)
    cos = jnp.repeat(jnp.cos(ang), 2, axis=-1)
    sin = jnp.stack([-jnp.sin(ang), jnp.sin(ang)], axis=-1).reshape(n_tokens, HEAD_DIM)
    return cos, sin


def kernel(x, c, ctx, c_ctx, ada_w, ada_b, norm_g, ffn_w_in, ffn_w_out, ssm_w_in, ssm_conv_w, ssm_conv_b,
           ssm_dt_bias, ssm_a_log, ssm_d, ssm_norm_g, ssm_w_out, pool_w, pool_scale, attn_w_qkv, attn_sink,
           attn_w_o, final_g):
    bsz, n_lat, _ = x.shape
    n_ctx = ctx.shape[1]
    depth = ada_w.shape[0]
    n_mixers = 3
    cond_rows = 2 * SUBLANES
    cond = jnp.zeros((cond_rows, D_MODEL), F32).at[:bsz].set(c).at[bsz].set(c_ctx)
    mods = adaln_all(cond, ada_w, ada_b)
    cos, sin = _rope_tables(n_lat)
    w_in = ffn_w_in.astype(BF16)
    w_out = ffn_w_out.astype(BF16)

    h = x.reshape(bsz * n_lat, D_MODEL)
    hc = ctx.reshape(bsz * n_ctx, D_MODEL)
    for i in range(depth):
        kind, j = i % n_mixers, i // n_mixers
        last = i == depth - 1
        ctx_live = (not last) or kind != 1
        ml = mods[i, :bsz].reshape(bsz, N_MOD, D_MODEL)
        mc = mods[i, bsz:bsz + 1].reshape(1, N_MOD, D_MODEL)

        h = ffn(h, ml, norm_g[i, 0], w_in, w_out, i, 0, 0, n_lat)
        if ctx_live:
            hc = ffn(hc, mc, norm_g[i, 0], w_in, w_out, i, 0, 0, bsz * n_ctx)

        if kind == 0:
            w_zx = ssm_w_in[j][:, :SSM_ZX].astype(BF16)
            w_dt = ssm_w_in[j][:, SSM_ZX:].astype(BF16)
            w_o = ssm_w_out[j].astype(BF16)
            proj = (norm_g[i, 1], w_zx, w_dt, ssm_conv_w[j], ssm_conv_b[j])
            zx_l, dt_l = ssm_in_proj(h.reshape(bsz, n_lat, D_MODEL), ml, *proj)
            zx_c, dt_c = ssm_in_proj(hc.reshape(bsz, n_ctx, D_MODEL), mc, *proj)
            zero = jnp.zeros((bsz, SSM_PAIRS, SSM_STATE, LANES), F32)
            fin = dict(d_skip=ssm_d[j], gn_g=ssm_norm_g[j])
            yf_c, sf = ssd_scan(zx_c, dt_c, ssm_dt_bias[j], ssm_a_log[j], zero, False)
            yn_c, sb = ssd_scan(zx_c, dt_c, ssm_dt_bias[j], ssm_a_log[j], zero, True, yf=yf_c, **fin)
            yf_l, _ = ssd_scan(zx_l, dt_l, ssm_dt_bias[j], ssm_a_log[j], sf, False)
            yn_l, _ = ssd_scan(zx_l, dt_l, ssm_dt_bias[j], ssm_a_log[j], sb, True, yf=yf_l, **fin)
            h = out_proj(yn_l.reshape(bsz * n_lat, SSM_INNER), w_o, h, ml, n_lat)
            if not last:
                hc = out_proj(yn_c.reshape(bsz * n_ctx, SSM_INNER), w_o, hc, mc, bsz * n_ctx)
        elif kind == 1:
            h = pool_mixer(h.reshape(bsz, n_lat, D_MODEL), ml, norm_g[i, 1], pool_w[j],
                           pool_scale[j]).reshape(bsz * n_lat, D_MODEL)
            if not last:
                hc = pool_mixer(hc.reshape(bsz, n_ctx, D_MODEL), mc, norm_g[i, 1], pool_w[j],
                                pool_scale[j]).reshape(bsz * n_ctx, D_MODEL)
        else:
            w_qkv = attn_w_qkv[j].astype(BF16)
            w_o = attn_w_o[j].astype(BF16)
            tn = 512
            qkv_l = mod_proj(h, ml, norm_g[i, 1], w_qkv, 1, n_lat, tn, BF16,
                             rope=(cos, sin, Q_DIM + KV_DIM)).reshape(bsz, n_lat, QKV_DIM)
            qkv_c = mod_proj(hc, mc, norm_g[i, 1], w_qkv, 1, bsz * n_ctx, tn, BF16).reshape(bsz, n_ctx, QKV_DIM)
            o_l = attention(qkv_l, qkv_c, attn_sink[j], True)
            h = out_proj(o_l.reshape(bsz * n_lat, Q_DIM), w_o, h, ml, n_lat)
            if not last:
                o_c = attention(qkv_c, qkv_c, attn_sink[j], False)
                hc = out_proj(o_c.reshape(bsz * n_ctx, Q_DIM), w_o, hc, mc, bsz * n_ctx)

        h = ffn(h, ml, norm_g[i, 2], w_in, w_out, i, 1, 2, n_lat, final_g=final_g if last else None)
        if not last:
            hc = ffn(hc, mc, norm_g[i, 2], w_in, w_out, i, 1, 2, bsz * n_ctx)
    return h.reshape(bsz, n_lat, D_MODEL)
```
